```python
import math
import jax, jax.numpy as jnp
from jax import lax
import numpy as np

D_MODEL = 1024
BATCH = 8
SEQ = 4096
DEPTH = 2

D_MIX = 2 * D_MODEL
SSM_W = D_MIX // 4
SSM_GROUP_CH = 16
SSM_GROUPS = SSM_W // SSM_GROUP_CH
SSM_STATE = 64
SSM_STEP_MIN = 1e-3
SSM_STEP_MAX = 1e-1
SGU_W = D_MIX // 4
CHUNK = 128
SGU_HEADS = 4
SGU_HEAD_W = SGU_W // SGU_HEADS
ATT_W = D_MIX // 2
ATT_HEAD_D = 64
ATT_HEADS = ATT_W // (2 * ATT_HEAD_D)
Q_BLOCK = 128
IN_SIZES = (SSM_W, SSM_W,
            SGU_W, SGU_W, SGU_W,
            ATT_W, ATT_W, ATT_W, ATT_W)
IN_COLS = sum(IN_SIZES)
NORM_EPS = 1e-6
LN_EPS = 1e-5

kernel_name = "hybrid_s5_sgu_diffattn_parallel_heads"


def rms_norm(x, g):
    xf = x.astype(jnp.float32)
    y = xf * lax.rsqrt(jnp.mean(xf * xf, axis=-1, keepdims=True) + NORM_EPS)
    return (y * g.astype(jnp.float32)).astype(x.dtype)


def layer_norm(x, g, b):
    xf = x.astype(jnp.float32)
    mu = jnp.mean(xf, axis=-1, keepdims=True)
    var = jnp.mean(jnp.square(xf - mu), axis=-1, keepdims=True)
    y = (xf - mu) * lax.rsqrt(var + LN_EPS)
    return (y * g.astype(jnp.float32) + b.astype(jnp.float32)).astype(x.dtype)


def _complex_affine_combine(e1, e2):
    a1r, a1i, b1r, b1i = e1
    a2r, a2i, b2r, b2i = e2
    ar = a2r * a1r - a2i * a1i
    ai = a2r * a1i + a2i * a1r
    br = a2r * b1r - a2i * b1i + b2r
    bi = a2r * b1i + a2i * b1r + b2i
    return (ar, ai, br, bi)


def s5_mixer(u, a_re, a_im, log_step, b_re, b_im, c_re, c_im, d_skip, glu_w, glu_b):
    bsz, seq, _ = u.shape
    f32 = jnp.float32
    uf = u.astype(f32).reshape(bsz, seq, SSM_GROUPS, SSM_GROUP_CH)
    lr = a_re.astype(f32)
    li = a_im.astype(f32)
    step = jnp.exp(log_step.astype(f32))[:, None]
    mag = jnp.exp(step * lr)
    ang = step * li
    ab_re = mag * jnp.cos(ang)
    ab_im = mag * jnp.sin(ang)
    den = lr * lr + li * li
    nr = ab_re - 1.0
    ni = ab_im
    co_re = (nr * lr + ni * li) / den
    co_im = (ni * lr - nr * li) / den
    br = b_re.astype(f32)
    bi = b_im.astype(f32)
    bb_re = co_re[..., None] * br - co_im[..., None] * bi
    bb_im = co_re[..., None] * bi + co_im[..., None] * br
    bu_re = jnp.einsum('blgh,gph->blgp', uf, bb_re)
    bu_im = jnp.einsum('blgh,gph->blgp', uf, bb_im)
    ar = jnp.broadcast_to(ab_re, bu_re.shape)
    ai = jnp.broadcast_to(ab_im, bu_im.shape)
    _, _, s_re, s_im = lax.associative_scan(_complex_affine_combine, (ar, ai, bu_re, bu_im), axis=1)
    y = (jnp.einsum('blgp,ghp->blgh', s_re, c_re.astype(f32))
         - jnp.einsum('blgp,ghp->blgh', s_im, c_im.astype(f32))
         + d_skip.astype(f32).reshape(SSM_GROUPS, SSM_GROUP_CH) * uf)
    y = y.reshape(bsz, seq, SSM_W)
    y = jax.nn.gelu(y, approximate=False)
    y = y * jax.nn.sigmoid(y @ glu_w.astype(f32) + glu_b.astype(f32))
    return y.astype(u.dtype)


def sgu_mixer(u, v, ln_g, ln_b, w_s, b_s):
    bsz, seq, _ = u.shape
    nc = seq // CHUNK
    vn = layer_norm(v, ln_g, ln_b).reshape(bsz, nc, CHUNK, SGU_HEADS, SGU_HEAD_W)
    causal = jnp.tril(jnp.ones((CHUNK, CHUNK), dtype=bool))
    w = jnp.where(causal[None], w_s, jnp.zeros_like(w_s))
    s = jnp.einsum('hts,bcshe->bcthe', w, vn) + b_s.T[:, :, None]
    return u * s.reshape(bsz, seq, SGU_W)


def diff_attention(q, k, v, lam, lam_init, subln_g):
    bsz, seq, nh, _, hd = q.shape
    nb = seq // Q_BLOCK
    qb = q.reshape(bsz, nb, Q_BLOCK, nh, 2, hd).transpose(1, 0, 2, 3, 4, 5)
    kpos = jnp.arange(seq)
    scale = hd ** -0.5
    neg = jnp.finfo(jnp.float32).min

    def block(args):
        qi, start = args
        s = jnp.einsum('bqhcd,bkhcd->bhcqk', qi.astype(jnp.float32), k.astype(jnp.float32)) * scale
        qpos = start + jnp.arange(Q_BLOCK)
        mask = kpos[None, :] <= qpos[:, None]
        p = jax.nn.softmax(jnp.where(mask, s, neg), axis=-1)
        wmap = p[:, :, 0] - lam * p[:, :, 1]
        return jnp.einsum('bhqk,bkhe->bqhe', wmap.astype(v.dtype), v)

    starts = jnp.arange(nb) * Q_BLOCK
    o = lax.map(block, (qb, starts))
    o = o.transpose(1, 0, 2, 3, 4).reshape(bsz, seq, nh, 2 * hd)
    o = rms_norm(o, subln_g) * (1.0 - lam_init)
    return o.reshape(bsz, seq, nh * 2 * hd)


def setup_inputs(seed: int = 0) -> dict:
    key = jax.random.key(seed)
    ks = jax.random.split(key, 24)
    f32 = jnp.float32
    nrm = lambda k, s: jax.random.normal(k, s, dtype=f32)
    n_idx = jnp.arange(SSM_STATE, dtype=f32)
    return {
        "x": nrm(ks[0], (BATCH, SEQ, D_MODEL)),
        "norm_g": 1.0 + 0.02 * nrm(ks[1], (DEPTH, D_MODEL)),
        "w_in": nrm(ks[2], (DEPTH, D_MODEL, IN_COLS)) * D_MODEL ** -0.5,
        "ssm_a_re": -0.5 + 0.01 * nrm(ks[3], (DEPTH, SSM_GROUPS, SSM_STATE)),
        "ssm_a_im": math.pi * n_idx + 0.01 * nrm(ks[4], (DEPTH, SSM_GROUPS, SSM_STATE)),
        "ssm_log_step": jax.random.uniform(ks[5], (DEPTH, SSM_GROUPS), dtype=f32,
                                           minval=math.log(SSM_STEP_MIN), maxval=math.log(SSM_STEP_MAX)),
        "ssm_b_re": nrm(ks[6], (DEPTH, SSM_GROUPS, SSM_STATE, SSM_GROUP_CH)) * (2 * SSM_GROUP_CH) ** -0.5,
        "ssm_b_im": nrm(ks[7], (DEPTH, SSM_GROUPS, SSM_STATE, SSM_GROUP_CH)) * (2 * SSM_GROUP_CH) ** -0.5,
        "ssm_c_re": nrm(ks[8], (DEPTH, SSM_GROUPS, SSM_GROUP_CH, SSM_STATE)) * SSM_STATE ** -0.5,
        "ssm_c_im": nrm(ks[9], (DEPTH, SSM_GROUPS, SSM_GROUP_CH, SSM_STATE)) * SSM_STATE ** -0.5,
        "ssm_d": nrm(ks[10], (DEPTH, SSM_W)),
        "glu_w": nrm(ks[11], (DEPTH, SSM_W, SSM_W)) * SSM_W ** -0.5,
        "glu_b": 0.01 * nrm(ks[12], (DEPTH, SSM_W)),
        "sgu_ln_g": 1.0 + 0.02 * nrm(ks[13], (DEPTH, SGU_W)),
        "sgu_ln_b": 0.01 * nrm(ks[14], (DEPTH, SGU_W)),
        "sgu_w": nrm(ks[15], (DEPTH, SGU_HEADS, CHUNK, CHUNK)) * 0.5 * CHUNK ** -0.5,
        "sgu_b": 1.0 + 0.01 * nrm(ks[16], (DEPTH, SGU_HEADS, CHUNK)),
        "lam_q1": 0.1 * nrm(ks[17], (DEPTH, ATT_HEAD_D)),
        "lam_k1": 0.1 * nrm(ks[18], (DEPTH, ATT_HEAD_D)),
        "lam_q2": 0.1 * nrm(ks[19], (DEPTH, ATT_HEAD_D)),
        "lam_k2": 0.1 * nrm(ks[20], (DEPTH, ATT_HEAD_D)),
        "attn_subln_g": 1.0 + 0.02 * nrm(ks[21], (DEPTH, 2 * ATT_HEAD_D)),
        "w_out": nrm(ks[22], (DEPTH, D_MIX, D_MODEL)) * D_MIX ** -0.5,
        "final_g": 1.0 + 0.02 * nrm(ks[23], (D_MODEL,)),
    }


def reference(x, norm_g, w_in, ssm_a_re, ssm_a_im, ssm_log_step, ssm_b_re, ssm_b_im,
              ssm_c_re, ssm_c_im, ssm_d, glu_w, glu_b, sgu_ln_g, sgu_ln_b, sgu_w, sgu_b,
              lam_q1, lam_k1, lam_q2, lam_k2, attn_subln_g, w_out, final_g):
    bsz, seq, _ = x.shape
    split_at = [sum(IN_SIZES[:i + 1]) for i in range(len(IN_SIZES) - 1)]
    for l in range(DEPTH):
        h = rms_norm(x, norm_g[l])
        z = h @ w_in[l]
        (s_in, s_gate, g_u, g_v, g_gate, a_q, a_k, a_v, a_gate) = jnp.split(z, split_at, axis=-1)

        y_ssm = s5_mixer(s_in, ssm_a_re[l], ssm_a_im[l], ssm_log_step[l], ssm_b_re[l], ssm_b_im[l],
                         ssm_c_re[l], ssm_c_im[l], ssm_d[l], glu_w[l], glu_b[l])
        y_ssm = y_ssm * jax.nn.silu(s_gate)

        y_sgu = sgu_mixer(g_u, g_v, sgu_ln_g[l], sgu_ln_b[l], sgu_w[l], sgu_b[l])
        y_sgu = y_sgu * jax.nn.silu(g_gate)

        lam_init = 0.8 - 0.6 * math.exp(-0.3 * l)
        lam = (jnp.exp(jnp.sum(lam_q1[l].astype(jnp.float32) * lam_k1[l].astype(jnp.float32)))
               - jnp.exp(jnp.sum(lam_q2[l].astype(jnp.float32) * lam_k2[l].astype(jnp.float32)))
               + lam_init)
        q = a_q.reshape(bsz, seq, ATT_HEADS, 2, ATT_HEAD_D)
        k = a_k.reshape(bsz, seq, ATT_HEADS, 2, ATT_HEAD_D)
        v = a_v.reshape(bsz, seq, ATT_HEADS, 2 * ATT_HEAD_D)
        y_att = diff_attention(q, k, v, lam, lam_init, attn_subln_g[l])
        y_att = y_att * jax.nn.silu(a_gate)

        mix = jnp.concatenate([y_ssm, y_sgu, y_att], axis=-1)
        x = x + mix @ w_out[l]
    return rms_norm(x, final_g)
```

```python
import functools
import math

import jax
import jax.numpy as jnp
from jax import lax
from jax.experimental import pallas as pl
from jax.experimental.pallas import tpu as pltpu

F32 = jnp.float32
BF16 = jnp.bfloat16

D_MODEL = 1024
D_MIX = 2 * D_MODEL
SSM_W = D_MIX // 4
SSM_GROUP_CH = 16
SSM_GROUPS = SSM_W // SSM_GROUP_CH
SSM_STATE = 64
SSM_CH = SSM_GROUPS * SSM_STATE
SSM_HALF_IN = SSM_W // 2
SSM_HALF_CH = SSM_CH // 2
SGU_W = D_MIX // 4
CHUNK = 128
SGU_HEADS = 4
SGU_HEAD_W = SGU_W // SGU_HEADS
ATT_W = D_MIX // 2
ATT_HEAD_D = 64
ATT_HEADS = ATT_W // (2 * ATT_HEAD_D)
ATT_HEAD_W = 2 * ATT_HEAD_D
NORM_EPS = 1e-6
LN_EPS = 1e-5

SSM_COLS = 2 * SSM_W
REST_COLS = 3 * SGU_W + 4 * ATT_W
SGU_U_BLK, SGU_V_BLK, SGU_G_BLK = 0, 1, 2
ATT_Q_OFF = 3 * SGU_W // ATT_HEAD_W
ATT_K_OFF = ATT_Q_OFF + ATT_HEADS
ATT_V_OFF = ATT_K_OFF + ATT_HEADS
ATT_G_OFF = ATT_V_OFF + ATT_HEADS

V7X_VMEM_LIMIT_BYTES = 56 * 1024 * 1024
MASK_VALUE = -1e30


def _rms(x, g):
    ms = jnp.mean(x * x, axis=-1, keepdims=True)
    return x * lax.rsqrt(ms + NORM_EPS) * g


def _silu(x):
    return x * (1.0 / (1.0 + jnp.exp(-x)))


def _sigmoid(x):
    return 1.0 / (1.0 + jnp.exp(-x))


def _gelu_exact(x):
    return 0.5 * x * (1.0 + lax.erf(x * math.sqrt(0.5)))


def _inproj_kernel(x_ref, g_ref, w_ref, o_ref, h_ref):
    @pl.when(pl.program_id(1) == 0)
    def _():
        h_ref[...] = _rms(x_ref[...], g_ref[...]).astype(BF16)

    o_ref[...] = jnp.dot(h_ref[...], w_ref[...], preferred_element_type=F32).astype(o_ref.dtype)


def _inproj(x2d, g, w, *, tm, tn):
    m, d = x2d.shape
    n = w.shape[1]
    return pl.pallas_call(
        _inproj_kernel,
        grid=(m // tm, n // tn),
        in_specs=[
            pl.BlockSpec((tm, d), lambda i, j: (i, 0)),
            pl.BlockSpec((1, d), lambda i, j: (0, 0)),
            pl.BlockSpec((d, tn), lambda i, j: (0, j)),
        ],
        out_specs=pl.BlockSpec((tm, tn), lambda i, j: (i, j)),
        out_shape=jax.ShapeDtypeStruct((m, n), BF16),
        scratch_shapes=[pltpu.VMEM((tm, d), BF16)],
        compiler_params=pltpu.CompilerParams(
            dimension_semantics=("parallel", "arbitrary"),
            vmem_limit_bytes=V7X_VMEM_LIMIT_BYTES),
        name="inproj",
    )(x2d, g, w)


def _ssm_kernel(x_ref, g_ref, w_ref, are_ref, aim_ref, bbd_ref, cre_ref, ncim_ref, d_ref,
                gw_ref, gb_ref, o_ref, bre_ref, bim_ref, carry_ref, *, tt, nb):
    @pl.when(pl.program_id(0) == 0)
    def _():
        carry_ref[...] = jnp.zeros_like(carry_ref)

    h = _rms(x_ref[...], g_ref[...]).astype(BF16)
    z = jnp.dot(h, w_ref[...], preferred_element_type=F32)
    u = z[:, :SSM_W]
    gate = z[:, SSM_W:]
    ub = u.astype(BF16)

    for hf in range(2):
        r = jnp.dot(ub[:, hf * SSM_HALF_IN:(hf + 1) * SSM_HALF_IN], bbd_ref[hf],
                    preferred_element_type=F32)
        bre_ref[:, hf * SSM_HALF_CH:(hf + 1) * SSM_HALF_CH] = r[:, :SSM_HALF_CH]
        bim_ref[:, hf * SSM_HALF_CH:(hf + 1) * SSM_HALF_CH] = r[:, SSM_HALF_CH:]

    def step(t, carry):
        sr, si = carry
        rows = pl.ds(pl.multiple_of(t * nb, nb), nb)
        ar = are_ref[...]
        ai = aim_ref[...]
        nsr = ar * sr - ai * si + bre_ref[rows, :]
        nsi = ar * si + ai * sr + bim_ref[rows, :]
        bre_ref[rows, :] = nsr
        bim_ref[rows, :] = nsi
        return nsr, nsi

    sr, si = lax.fori_loop(0, tt, step, (carry_ref[0], carry_ref[1]))
    carry_ref[0] = sr
    carry_ref[1] = si

    ys = []
    for hf in range(2):
        cols = slice(hf * SSM_HALF_CH, (hf + 1) * SSM_HALF_CH)
        yh = jnp.dot(bre_ref[:, cols].astype(BF16), cre_ref[hf], preferred_element_type=F32)
        yh = yh + jnp.dot(bim_ref[:, cols].astype(BF16), ncim_ref[hf], preferred_element_type=F32)
        ys.append(yh)
    y = jnp.concatenate(ys, axis=-1) + d_ref[...] * u
    y = _gelu_exact(y)
    y = y * _sigmoid(jnp.dot(y.astype(BF16), gw_ref[...], preferred_element_type=F32) + gb_ref[...])
    o_ref[...] = (y * _silu(gate)).astype(o_ref.dtype)


def _ssm_branch(x_tb, g, w_ssm, are8, aim8, bbd, cre_bd, ncim_bd, d, glu_w, glu_b, *, tt, nb):
    m, dm = x_tb.shape
    r = tt * nb
    const = lambda *shape: pl.BlockSpec(shape, lambda i: (0,) * len(shape))
    return pl.pallas_call(
        functools.partial(_ssm_kernel, tt=tt, nb=nb),
        grid=(m // r,),
        in_specs=[
            pl.BlockSpec((r, dm), lambda i: (i, 0)),
            const(1, dm),
            const(dm, SSM_COLS),
            const(nb, SSM_CH),
            const(nb, SSM_CH),
            const(2, SSM_HALF_IN, 2 * SSM_HALF_CH),
            const(2, SSM_HALF_CH, SSM_HALF_IN),
            const(2, SSM_HALF_CH, SSM_HALF_IN),
            const(1, SSM_W),
            const(SSM_W, SSM_W),
            const(1, SSM_W),
        ],
        out_specs=pl.BlockSpec((r, SSM_W), lambda i: (i, 0)),
        out_shape=jax.ShapeDtypeStruct((m, SSM_W), BF16),
        scratch_shapes=[
            pltpu.VMEM((r, SSM_CH), F32),
            pltpu.VMEM((r, SSM_CH), F32),
            pltpu.VMEM((2, nb, SSM_CH), F32),
        ],
        compiler_params=pltpu.CompilerParams(
            dimension_semantics=("arbitrary",),
            vmem_limit_bytes=V7X_VMEM_LIMIT_BYTES),
        name="ssm_branch",
    )(x_tb, g, w_ssm, are8, aim8, bbd, cre_bd, ncim_bd, d, glu_w, glu_b)


def _ssm_params(a_re, a_im, log_step, b_re, b_im, c_re, c_im, nb):
    lr = a_re.astype(F32)
    li = a_im.astype(F32)
    step = jnp.exp(log_step.astype(F32))[:, None]
    mag = jnp.exp(step * lr)
    ang = step * li
    ab_re = mag * jnp.cos(ang)
    ab_im = mag * jnp.sin(ang)
    den = lr * lr + li * li
    nr = ab_re - 1.0
    ni = ab_im
    co_re = (nr * lr + ni * li) / den
    co_im = (ni * lr - nr * li) / den
    br = b_re.astype(F32)
    bi = b_im.astype(F32)
    bb_re = co_re[..., None] * br - co_im[..., None] * bi
    bb_im = co_re[..., None] * bi + co_im[..., None] * br
    gh = SSM_GROUPS // 2
    eye = jnp.eye(gh, dtype=F32)

    def b_blockdiag(bb):
        t = bb.reshape(2, gh, SSM_STATE, SSM_GROUP_CH)
        t = jnp.einsum('fgph,gk->fghkp', t, eye)
        return t.reshape(2, gh * SSM_GROUP_CH, gh * SSM_STATE)

    def c_blockdiag(c):
        t = c.astype(F32).reshape(2, gh, SSM_GROUP_CH, SSM_STATE)
        t = jnp.einsum('fghp,gk->fgpkh', t, eye)
        return t.reshape(2, gh * SSM_STATE, gh * SSM_GROUP_CH)

    bbd = jnp.concatenate([b_blockdiag(bb_re), b_blockdiag(bb_im)], axis=-1).astype(BF16)
    cre_bd = c_blockdiag(c_re).astype(BF16)
    ncim_bd = (-c_blockdiag(c_im)).astype(BF16)
    are8 = jnp.broadcast_to(ab_re.reshape(1, SSM_CH), (nb, SSM_CH))
    aim8 = jnp.broadcast_to(ab_im.reshape(1, SSM_CH), (nb, SSM_CH))
    return are8, aim8, bbd, cre_bd, ncim_bd


def _sgu_kernel(u_ref, v_ref, gt_ref, lg_ref, lb_ref, w_ref, bt_ref, o_ref, *, tm):
    v = v_ref[...].astype(F32)
    mu = jnp.mean(v, axis=-1, keepdims=True)
    vc = v - mu
    var = jnp.mean(vc * vc, axis=-1, keepdims=True)
    vn = (vc * lax.rsqrt(var + LN_EPS) * lg_ref[...] + lb_ref[...]).astype(BF16)
    row = lax.broadcasted_iota(jnp.int32, (CHUNK, CHUNK), 0)
    col = lax.broadcasted_iota(jnp.int32, (CHUNK, CHUNK), 1)
    for hd in range(SGU_HEADS):
        w = jnp.where(col <= row, w_ref[hd], 0.0).astype(BF16)
        bias = bt_ref[:, hd:hd + 1]
        cs = slice(hd * SGU_HEAD_W, (hd + 1) * SGU_HEAD_W)
        for c in range(tm // CHUNK):
            rs = slice(c * CHUNK, (c + 1) * CHUNK)
            s = jnp.dot(w, vn[rs, cs], preferred_element_type=F32) + bias
            uu = u_ref[rs, cs].astype(F32)
            gg = gt_ref[rs, cs].astype(F32)
            o_ref[rs, cs] = (uu * s * _silu(gg)).astype(o_ref.dtype)


def _sgu_branch(z_rest, ln_g, ln_b, w_s, b_t, *, tm):
    m = z_rest.shape[0]
    const = lambda *shape: pl.BlockSpec(shape, lambda i: (0,) * len(shape))
    return pl.pallas_call(
        functools.partial(_sgu_kernel, tm=tm),
        grid=(m // tm,),
        in_specs=[
            pl.BlockSpec((tm, SGU_W), lambda i: (i, SGU_U_BLK)),
            pl.BlockSpec((tm, SGU_W), lambda i: (i, SGU_V_BLK)),
            pl.BlockSpec((tm, SGU_W), lambda i: (i, SGU_G_BLK)),
            const(1, SGU_W),
            const(1, SGU_W),
            const(SGU_HEADS, CHUNK, CHUNK),
            const(CHUNK, SGU_HEADS),
        ],
        out_specs=pl.BlockSpec((tm, SGU_W), lambda i: (i, 0)),
        out_shape=jax.ShapeDtypeStruct((m, SGU_W), BF16),
        compiler_params=pltpu.CompilerParams(dimension_semantics=("parallel",)),
        name="sgu_branch",
    )(z_rest, z_rest, z_rest, ln_g, ln_b, w_s, b_t)


def _attn_kernel(q_ref, k_ref, v_ref, gt_ref, lq1_ref, lk1_ref, lq2_ref, lk2_ref, sg_ref,
                 o_ref, qz_ref, *, tq, lam_init):
    qi = pl.program_id(2)
    q = q_ref[...] * ATT_HEAD_D ** -0.5
    lane = lax.broadcasted_iota(jnp.int32, (tq, ATT_HEAD_W), 1)
    qz_ref[:tq, :] = jnp.where(lane < ATT_HEAD_D, q, 0).astype(BF16)
    qz_ref[tq:, :] = jnp.where(lane >= ATT_HEAD_D, q, 0).astype(BF16)

    def sweep(kb, carry, masked):
        m, l, acc = carry
        rows = pl.ds(pl.multiple_of(kb * tq, tq), tq)
        kblk = k_ref[rows, :]
        vblk = v_ref[rows, :]
        s = lax.dot_general(qz_ref[...], kblk, (((1,), (1,)), ((), ())),
                            preferred_element_type=F32)
        if masked:
            r = lax.broadcasted_iota(jnp.int32, (2 * tq, tq), 0)
            c = lax.broadcasted_iota(jnp.int32, (2 * tq, tq), 1)
            r = jnp.where(r >= tq, r - tq, r)
            s = jnp.where(c <= r, s, MASK_VALUE)
        m_new = jnp.maximum(m, jnp.max(s, axis=-1, keepdims=True))
        alpha = jnp.exp(m - m_new)
        p = jnp.exp(s - m_new)
        l = alpha * l + jnp.sum(p, axis=-1, keepdims=True)
        acc = alpha * acc + jnp.dot(p.astype(BF16), vblk, preferred_element_type=F32)
        return m_new, l, acc

    init = (jnp.full((2 * tq, 1), MASK_VALUE, F32), jnp.zeros((2 * tq, 1), F32),
            jnp.zeros((2 * tq, ATT_HEAD_W), F32))
    carry = lax.fori_loop(0, qi, functools.partial(sweep, masked=False), init)
    m, l, acc = sweep(qi, carry, True)

    lam = (jnp.exp(jnp.sum(lq1_ref[...] * lk1_ref[...], axis=-1, keepdims=True))
           - jnp.exp(jnp.sum(lq2_ref[...] * lk2_ref[...], axis=-1, keepdims=True)) + lam_init)
    o = acc[:tq] / l[:tq] - lam * (acc[tq:] / l[tq:])
    o = _rms(o, sg_ref[...]) * (1.0 - lam_init)
    o_ref[...] = (o * _silu(gt_ref[...].astype(F32))).astype(o_ref.dtype)


def _attention(z_rest, lq1, lk1, lq2, lk2, subln_g, *, nbatch, seq, tq, lam_init):
    nq = seq // tq
    const = lambda *shape: pl.BlockSpec(shape, lambda b, h, i: (0,) * len(shape))
    return pl.pallas_call(
        functools.partial(_attn_kernel, tq=tq, lam_init=lam_init),
        grid=(nbatch, ATT_HEADS, nq),
        in_specs=[
            pl.BlockSpec((tq, ATT_HEAD_W), lambda b, h, i: (b * nq + i, ATT_Q_OFF + h)),
            pl.BlockSpec((seq, ATT_HEAD_W), lambda b, h, i: (b, ATT_K_OFF + h)),
            pl.BlockSpec((seq, ATT_HEAD_W), lambda b, h, i: (b, ATT_V_OFF + h)),
            pl.BlockSpec((tq, ATT_HEAD_W), lambda b, h, i: (b * nq + i, ATT_G_OFF + h)),
            const(1, ATT_HEAD_D),
            const(1, ATT_HEAD_D),
            const(1, ATT_HEAD_D),
            const(1, ATT_HEAD_D),
            const(1, ATT_HEAD_W),
        ],
        out_specs=pl.BlockSpec((tq, ATT_HEAD_W), lambda b, h, i: (b * nq + i, h)),
        out_shape=jax.ShapeDtypeStruct((nbatch * seq, ATT_W), BF16),
        scratch_shapes=[pltpu.VMEM((2 * tq, ATT_HEAD_W), BF16)],
        compiler_params=pltpu.CompilerParams(
            dimension_semantics=("parallel", "parallel", "arbitrary"),
            vmem_limit_bytes=V7X_VMEM_LIMIT_BYTES),
        name="diff_attention",
    )(z_rest, z_rest, z_rest, z_rest, lq1, lk1, lq2, lk2, subln_g)


def _outproj_kernel(ys_ref, yg_ref, ya_ref, w_ref, x_ref, fg_ref, o_ref, *, final_norm):
    acc = jnp.dot(ys_ref[...], w_ref[:SSM_W, :], preferred_element_type=F32)
    acc = acc + jnp.dot(yg_ref[...], w_ref[SSM_W:SSM_W + SGU_W, :], preferred_element_type=F32)
    acc = acc + jnp.dot(ya_ref[...], w_ref[SSM_W + SGU_W:, :], preferred_element_type=F32)
    xn = x_ref[...] + acc
    if final_norm:
        xn = _rms(xn, fg_ref[...])
    o_ref[...] = xn


def _outproj(y_ssm, y_sgu, y_att, w_out, x2d, final_g, *, tm, final_norm):
    m, d = x2d.shape
    const = lambda *shape: pl.BlockSpec(shape, lambda i: (0,) * len(shape))
    return pl.pallas_call(
        functools.partial(_outproj_kernel, final_norm=final_norm),
        grid=(m // tm,),
        in_specs=[
            pl.BlockSpec((tm, SSM_W), lambda i: (i, 0)),
            pl.BlockSpec((tm, SGU_W), lambda i: (i, 0)),
            pl.BlockSpec((tm, ATT_W), lambda i: (i, 0)),
            const(D_MIX, d),
            pl.BlockSpec((tm, d), lambda i: (i, 0)),
            const(1, d),
        ],
        out_specs=pl.BlockSpec((tm, d), lambda i: (i, 0)),
        out_shape=jax.ShapeDtypeStruct((m, d), F32),
        compiler_params=pltpu.CompilerParams(
            dimension_semantics=("parallel",),
            vmem_limit_bytes=V7X_VMEM_LIMIT_BYTES),
        name="outproj",
    )(y_ssm, y_sgu, y_att, w_out, x2d, final_g)


def _pick(n, prefs):
    for p in prefs:
        if n % p == 0:
            return p
    raise ValueError(f"no supported tile for extent {n}")


def kernel(x, norm_g, w_in, ssm_a_re, ssm_a_im, ssm_log_step, ssm_b_re, ssm_b_im, ssm_c_re, ssm_c_im,
           ssm_d, glu_w, glu_b, sgu_ln_g, sgu_ln_b, sgu_w, sgu_b, lam_q1, lam_k1, lam_q2, lam_k2,
           attn_subln_g, w_out, final_g):
    nbatch, seq, d = x.shape
    depth = norm_g.shape[0]
    assert d == D_MODEL and seq % 256 == 0 and nbatch == 8
    m = nbatch * seq
    tm_in = _pick(m, (1024, 512, 256))
    tn_in = REST_COLS // 4
    tt = _pick(seq, (64, 32))
    tq = 256

    x2d = x.reshape(m, d).astype(F32)
    fg = final_g.reshape(1, d).astype(F32)
    for l in range(depth):
        g = norm_g[l].reshape(1, d).astype(F32)
        w_l = w_in[l].astype(BF16)
        lam_init = 0.8 - 0.6 * math.exp(-0.3 * l)

        z_rest = _inproj(x2d, g, w_l[:, SSM_COLS:], tm=tm_in, tn=tn_in)

        ssm_p = _ssm_params(ssm_a_re[l], ssm_a_im[l], ssm_log_step[l], ssm_b_re[l], ssm_b_im[l],
                            ssm_c_re[l], ssm_c_im[l], nbatch)
        x_tb = x2d.reshape(nbatch, seq, d).swapaxes(0, 1).reshape(m, d)
        y_ssm_tb = _ssm_branch(x_tb, g, w_l[:, :SSM_COLS], *ssm_p,
                               ssm_d[l].reshape(1, SSM_W).astype(F32), glu_w[l].astype(BF16),
                               glu_b[l].reshape(1, SSM_W).astype(F32), tt=tt, nb=nbatch)
        y_ssm = y_ssm_tb.reshape(seq, nbatch, SSM_W).swapaxes(0, 1).reshape(m, SSM_W)

        y_sgu = _sgu_branch(z_rest, sgu_ln_g[l].reshape(1, SGU_W).astype(F32),
                            sgu_ln_b[l].reshape(1, SGU_W).astype(F32), sgu_w[l].astype(F32),
                            sgu_b[l].astype(F32).T, tm=512)

        y_att = _attention(z_rest, lam_q1[l].reshape(1, -1).astype(F32), lam_k1[l].reshape(1, -1).astype(F32),
                           lam_q2[l].reshape(1, -1).astype(F32), lam_k2[l].reshape(1, -1).astype(F32),
                           attn_subln_g[l].reshape(1, ATT_HEAD_W).astype(F32),
                           nbatch=nbatch, seq=seq, tq=tq, lam_init=lam_init)

        x2d = _outproj(y_ssm, y_sgu, y_att, w_out[l].astype(BF16), x2d, fg,
                       tm=512, final_norm=(l == depth - 1))
    return x2d.reshape(nbatch, seq, d).astype(x.dtype)
```

```python
import functools
import math

import jax
import jax.numpy as jnp
from jax import lax
from jax.experimental import pallas as pl
from jax.experimental.pallas import tpu as pltpu

F32 = jnp.float32
BF16 = jnp.bfloat16

D_MODEL = 1024
D_MIX = 2 * D_MODEL
SSM_W = D_MIX // 4
SSM_GROUP_CH = 16
SSM_GROUPS = SSM_W // SSM_GROUP_CH
SSM_STATE = 64
SSM_CH = SSM_GROUPS * SSM_STATE
SSM_HALF_IN = SSM_W // 2
SSM_HALF_CH = SSM_CH // 2
SGU_W = D_MIX // 4
CHUNK = 128
SGU_HEADS = 4
SGU_HEAD_W = SGU_W // SGU_HEADS
ATT_W = D_MIX // 2
ATT_HEAD_D = 64
ATT_HEADS = ATT_W // (2 * ATT_HEAD_D)
ATT_HEAD_W = 2 * ATT_HEAD_D
NORM_EPS = 1e-6
LN_EPS = 1e-5

SSM_COLS = 2 * SSM_W
REST_COLS = 3 * SGU_W + 4 * ATT_W
SGU_U_BLK, SGU_V_BLK, SGU_G_BLK = 0, 1, 2
ATT_Q_OFF = 3 * SGU_W // ATT_HEAD_W
ATT_K_OFF = ATT_Q_OFF + ATT_HEADS
ATT_V_OFF = ATT_K_OFF + ATT_HEADS
ATT_G_OFF = ATT_V_OFF + ATT_HEADS

V7X_VMEM_LIMIT_BYTES = 56 * 1024 * 1024
MASK_VALUE = -1e30


def _rms(x, g):
    ms = jnp.mean(x * x, axis=-1, keepdims=True)
    return x * lax.rsqrt(ms + NORM_EPS) * g


def _silu(x):
    return x * (1.0 / (1.0 + jnp.exp(-x)))


def _sigmoid(x):
    return 1.0 / (1.0 + jnp.exp(-x))


def _gelu_exact(x):
    return 0.5 * x * (1.0 + lax.erf(x * math.sqrt(0.5)))


def _inproj_kernel(x_ref, g_ref, w_ref, o_ref, h_ref):
    @pl.when(pl.program_id(1) == 0)
    def _():
        h_ref[...] = _rms(x_ref[...], g_ref[...]).astype(BF16)

    o_ref[...] = jnp.dot(h_ref[...], w_ref[...], preferred_element_type=F32).astype(o_ref.dtype)


def _inproj(x2d, g, w, *, tm, tn):
    m, d = x2d.shape
    n = w.shape[1]
    return pl.pallas_call(
        _inproj_kernel,
        grid=(m // tm, n // tn),
        in_specs=[
            pl.BlockSpec((tm, d), lambda i, j: (i, 0)),
            pl.BlockSpec((1, d), lambda i, j: (0, 0)),
            pl.BlockSpec((d, tn), lambda i, j: (0, j)),
        ],
        out_specs=pl.BlockSpec((tm, tn), lambda i, j: (i, j)),
        out_shape=jax.ShapeDtypeStruct((m, n), BF16),
        scratch_shapes=[pltpu.VMEM((tm, d), BF16)],
        compiler_params=pltpu.CompilerParams(
            dimension_semantics=("parallel", "arbitrary"),
            vmem_limit_bytes=V7X_VMEM_LIMIT_BYTES),
        name="inproj",
    )(x2d, g, w)


def _ssm_kernel(x_ref, g_ref, w_ref, are_ref, aim_ref, bbd_ref, cre_ref, ncim_ref, d_ref,
                gw_ref, gb_ref, o_ref, bre_ref, bim_ref, carry_ref, *, tt, nb):
    @pl.when(pl.program_id(0) == 0)
    def _():
        carry_ref[...] = jnp.zeros_like(carry_ref)

    h = _rms(x_ref[...], g_ref[...]).astype(BF16)
    z = jnp.dot(h, w_ref[...], preferred_element_type=F32)
    u = z[:, :SSM_W]
    gate = z[:, SSM_W:]
    ub = u.astype(BF16)

    for hf in range(2):
        r = jnp.dot(ub[:, hf * SSM_HALF_IN:(hf + 1) * SSM_HALF_IN], bbd_ref[hf],
                    preferred_element_type=F32)
        bre_ref[:, hf * SSM_HALF_CH:(hf + 1) * SSM_HALF_CH] = r[:, :SSM_HALF_CH]
        bim_ref[:, hf * SSM_HALF_CH:(hf + 1) * SSM_HALF_CH] = r[:, SSM_HALF_CH:]

    def step(t, carry):
        sr, si = carry
        rows = pl.ds(pl.multiple_of(t * nb, nb), nb)
        ar = are_ref[...]
        ai = aim_ref[...]
        nsr = ar * sr - ai * si + bre_ref[rows, :]
        nsi = ar * si + ai * sr + bim_ref[rows, :]
        bre_ref[rows, :] = nsr
        bim_ref[rows, :] = nsi
        return nsr, nsi

    sr, si = lax.fori_loop(0, tt, step, (carry_ref[0], carry_ref[1]))
    carry_ref[0] = sr
    carry_ref[1] = si

    ys = []
    for hf in range(2):
        cols = slice(hf * SSM_HALF_CH, (hf + 1) * SSM_HALF_CH)
        yh = jnp.dot(bre_ref[:, cols].astype(BF16), cre_ref[hf], preferred_element_type=F32)
        yh = yh + jnp.dot(bim_ref[:, cols].astype(BF16), ncim_ref[hf], preferred_element_type=F32)
        ys.append(yh)
    y = jnp.concatenate(ys, axis=-1) + d_ref[...] * u
    y = _gelu_exact(y)
    y = y * _sigmoid(jnp.dot(y.astype(BF16), gw_ref[...], preferred_element_type=F32) + gb_ref[...])
    o_ref[...] = (y * _silu(gate)).astype(o_ref.dtype)


def _ssm_branch(x_tb, g, w_ssm, are8, aim8, bbd, cre_bd, ncim_bd, d, glu_w, glu_b, *, tt, nb):
    m, dm = x_tb.shape
    r = tt * nb
    const = lambda *shape: pl.BlockSpec(shape, lambda i: (0,) * len(shape))
    return pl.pallas_call(
        functools.partial(_ssm_kernel, tt=tt, nb=nb),
        grid=(m // r,),
        in_specs=[
            pl.BlockSpec((r, dm), lambda i: (i, 0)),
            const(1, dm),
            const(dm, SSM_COLS),
            const(nb, SSM_CH),
            const(nb, SSM_CH),
            const(2, SSM_HALF_IN, 2 * SSM_HALF_CH),
            const(2, SSM_HALF_CH, SSM_HALF_IN),
            const(2, SSM_HALF_CH, SSM_HALF_IN),
            const(1, SSM_W),
            const(SSM_W, SSM_W),
            const(1, SSM_W),
        ],
        out_specs=pl.BlockSpec((r, SSM_W), lambda i: (i, 0)),
        out_shape=jax.ShapeDtypeStruct((m, SSM_W), BF16),
        scratch_shapes=[
            pltpu.VMEM((r, SSM_CH), F32),
            pltpu.VMEM((r, SSM_CH), F32),
            pltpu.VMEM((2, nb, SSM_CH), F32),
        ],
        compiler_params=pltpu.CompilerParams(
            dimension_semantics=("arbitrary",),
            vmem_limit_bytes=V7X_VMEM_LIMIT_BYTES),
        name="ssm_branch",
    )(x_tb, g, w_ssm, are8, aim8, bbd, cre_bd, ncim_bd, d, glu_w, glu_b)


def _ssm_params(a_re, a_im, log_step, b_re, b_im, c_re, c_im, nb):
    lr = a_re.astype(F32)
    li = a_im.astype(F32)
    step = jnp.exp(log_step.astype(F32))[:, None]
    mag = jnp.exp(step * lr)
    ang = step * li
    ab_re = mag * jnp.cos(ang)
    ab_im = mag * jnp.sin(ang)
    den = lr * lr + li * li
    nr = ab_re - 1.0
    ni = ab_im
    co_re = (nr * lr + ni * li) / den
    co_im = (ni * lr - nr * li) / den
    br = b_re.astype(F32)
    bi = b_im.astype(F32)
    bb_re = co_re[..., None] * br - co_im[..., None] * bi
    bb_im = co_re[..., None] * bi + co_im[..., None] * br
    gh = SSM_GROUPS // 2
    eye = jnp.eye(gh, dtype=F32)

    def b_blockdiag(bb):
        t = bb.reshape(2, gh, SSM_STATE, SSM_GROUP_CH)
        t = jnp.einsum('fgph,gk->fghkp', t, eye)
        return t.reshape(2, gh * SSM_GROUP_CH, gh * SSM_STATE)

    def c_blockdiag(c):
        t = c.astype(F32).reshape(2, gh, SSM_GROUP_CH, SSM_STATE)
        t = jnp.einsum('fghp,gk->fgpkh', t, eye)
        return t.reshape(2, gh * SSM_STATE, gh * SSM_GROUP_CH)

    bbd = jnp.concatenate([b_blockdiag(bb_re), b_blockdiag(bb_im)], axis=-1).astype(BF16)
    cre_bd = c_blockdiag(c_re).astype(BF16)
    ncim_bd = (-c_blockdiag(c_im)).astype(BF16)
    are8 = jnp.broadcast_to(ab_re.reshape(1, SSM_CH), (nb, SSM_CH))
    aim8 = jnp.broadcast_to(ab_im.reshape(1, SSM_CH), (nb, SSM_CH))
    return are8, aim8, bbd, cre_bd, ncim_bd


def _sgu_kernel(u_ref, v_ref, gt_ref, lg_ref, lb_ref, w_ref, bt_ref, o_ref, *, tm):
    v = v_ref[...].astype(F32)
    mu = jnp.mean(v, axis=-1, keepdims=True)
    vc = v - mu
    var = jnp.mean(vc * vc, axis=-1, keepdims=True)
    vn = (vc * lax.rsqrt(var + LN_EPS) * lg_ref[...] + lb_ref[...]).astype(BF16)
    row = lax.broadcasted_iota(jnp.int32, (CHUNK, CHUNK), 0)
    col = lax.broadcasted_iota(jnp.int32, (CHUNK, CHUNK), 1)
    for hd in range(SGU_HEADS):
        w = jnp.where(col <= row, w_ref[hd], 0.0).astype(BF16)
        bias = bt_ref[:, hd:hd + 1]
        cs = slice(hd * SGU_HEAD_W, (hd + 1) * SGU_HEAD_W)
        for c in range(tm // CHUNK):
            rs = slice(c * CHUNK, (c + 1) * CHUNK)
            s = jnp.dot(w, vn[rs, cs], preferred_element_type=F32) + bias
            uu = u_ref[rs, cs].astype(F32)
            gg = gt_ref[rs, cs].astype(F32)
            o_ref[rs, cs] = (uu * s * _silu(gg)).astype(o_ref.dtype)


def _sgu_branch(z_rest, ln_g, ln_b, w_s, b_t, *, tm):
    m = z_rest.shape[0]
    const = lambda *shape: pl.BlockSpec(shape, lambda i: (0,) * len(shape))
    return pl.pallas_call(
        functools.partial(_sgu_kernel, tm=tm),
        grid=(m // tm,),
        in_specs=[
            pl.BlockSpec((tm, SGU_W), lambda i: (i, SGU_U_BLK)),
            pl.BlockSpec((tm, SGU_W), lambda i: (i, SGU_V_BLK)),
            pl.BlockSpec((tm, SGU_W), lambda i: (i, SGU_G_BLK)),
            const(1, SGU_W),
            const(1, SGU_W),
            const(SGU_HEADS, CHUNK, CHUNK),
            const(CHUNK, SGU_HEADS),
        ],
        out_specs=pl.BlockSpec((tm, SGU_W), lambda i: (i, 0)),
        out_shape=jax.ShapeDtypeStruct((m, SGU_W), BF16),
        compiler_params=pltpu.CompilerParams(dimension_semantics=("parallel",)),
        name="sgu_branch",
    )(z_rest, z_rest, z_rest, ln_g, ln_b, w_s, b_t)


def _attn_kernel(q_ref, k_ref, v_ref, gt_ref, lq1_ref, lk1_ref, lq2_ref, lk2_ref, sg_ref,
                 o_ref, qzt_ref, vt_ref, *, tq, lam_init):
    qi = pl.program_id(2)
    nkb = vt_ref.shape[0]

    @pl.when(qi == 0)
    def _():
        for j in range(nkb):
            vt_ref[j] = v_ref[j * tq:(j + 1) * tq, :].T

    qt = (q_ref[...] * ATT_HEAD_D ** -0.5).T
    sub = lax.broadcasted_iota(jnp.int32, (ATT_HEAD_W, tq), 0)
    qzt_ref[:, :tq] = jnp.where(sub < ATT_HEAD_D, qt, 0).astype(BF16)
    qzt_ref[:, tq:] = jnp.where(sub >= ATT_HEAD_D, qt, 0).astype(BF16)

    def sweep(kb, carry, masked):
        m, l, acc = carry
        rows = pl.ds(pl.multiple_of(kb * tq, tq), tq)
        s = jnp.dot(k_ref[rows, :], qzt_ref[...], preferred_element_type=F32)
        if masked:
            key = lax.broadcasted_iota(jnp.int32, (tq, 2 * tq), 0)
            qry = lax.broadcasted_iota(jnp.int32, (tq, 2 * tq), 1)
            qry = jnp.where(qry >= tq, qry - tq, qry)
            s = jnp.where(key <= qry, s, MASK_VALUE)
        m_new = jnp.maximum(m, jnp.max(s, axis=0, keepdims=True))
        alpha = jnp.exp(m - m_new)
        p = jnp.exp(s - m_new)
        l = alpha * l + jnp.sum(p, axis=0, keepdims=True)
        acc = alpha * acc + jnp.dot(vt_ref[kb], p.astype(BF16), preferred_element_type=F32)
        return m_new, l, acc

    init = (jnp.full((1, 2 * tq), MASK_VALUE, F32), jnp.zeros((1, 2 * tq), F32),
            jnp.zeros((ATT_HEAD_W, 2 * tq), F32))
    carry = lax.fori_loop(0, qi, functools.partial(sweep, masked=False), init)
    m, l, acc = sweep(qi, carry, True)

    lam = (jnp.exp(jnp.sum(lq1_ref[...] * lk1_ref[...], axis=-1, keepdims=True))
           - jnp.exp(jnp.sum(lq2_ref[...] * lk2_ref[...], axis=-1, keepdims=True)) + lam_init)
    ot = acc[:, :tq] / l[:, :tq] - lam * (acc[:, tq:] / l[:, tq:])
    o = _rms(ot.T, sg_ref[...]) * (1.0 - lam_init)
    o_ref[...] = (o * _silu(gt_ref[...].astype(F32))).astype(o_ref.dtype)


def _attention(z_rest, lq1, lk1, lq2, lk2, subln_g, *, nbatch, seq, tq, lam_init):
    nq = seq // tq
    const = lambda *shape: pl.BlockSpec(shape, lambda b, h, i: (0,) * len(shape))
    return pl.pallas_call(
        functools.partial(_attn_kernel, tq=tq, lam_init=lam_init),
        grid=(nbatch, ATT_HEADS, nq),
        in_specs=[
            pl.BlockSpec((tq, ATT_HEAD_W), lambda b, h, i: (b * nq + i, ATT_Q_OFF + h)),
            pl.BlockSpec((seq, ATT_HEAD_W), lambda b, h, i: (b, ATT_K_OFF + h)),
            pl.BlockSpec((seq, ATT_HEAD_W), lambda b, h, i: (b, ATT_V_OFF + h)),
            pl.BlockSpec((tq, ATT_HEAD_W), lambda b, h, i: (b * nq + i, ATT_G_OFF + h)),
            const(1, ATT_HEAD_D),
            const(1, ATT_HEAD_D),
            const(1, ATT_HEAD_D),
            const(1, ATT_HEAD_D),
            const(1, ATT_HEAD_W),
        ],
        out_specs=pl.BlockSpec((tq, ATT_HEAD_W), lambda b, h, i: (b * nq + i, h)),
        out_shape=jax.ShapeDtypeStruct((nbatch * seq, ATT_W), BF16),
        scratch_shapes=[pltpu.VMEM((ATT_HEAD_W, 2 * tq), BF16),
                        pltpu.VMEM((nq, ATT_HEAD_W, tq), BF16)],
        compiler_params=pltpu.CompilerParams(
            dimension_semantics=("parallel", "parallel", "arbitrary"),
            vmem_limit_bytes=V7X_VMEM_LIMIT_BYTES),
        name="diff_attention",
    )(z_rest, z_rest, z_rest, z_rest, lq1, lk1, lq2, lk2, subln_g)


def _outproj_kernel(ys_ref, yg_ref, ya_ref, w_ref, x_ref, fg_ref, o_ref, *, final_norm):
    acc = jnp.dot(ys_ref[...], w_ref[:SSM_W, :], preferred_element_type=F32)
    acc = acc + jnp.dot(yg_ref[...], w_ref[SSM_W:SSM_W + SGU_W, :], preferred_element_type=F32)
    acc = acc + jnp.dot(ya_ref[...], w_ref[SSM_W + SGU_W:, :], preferred_element_type=F32)
    xn = x_ref[...] + acc
    if final_norm:
        xn = _rms(xn, fg_ref[...])
    o_ref[...] = xn


def _outproj(y_ssm, y_sgu, y_att, w_out, x2d, final_g, *, tm, final_norm):
    m, d = x2d.shape
    const = lambda *shape: pl.BlockSpec(shape, lambda i: (0,) * len(shape))
    return pl.pallas_call(
        functools.partial(_outproj_kernel, final_norm=final_norm),
        grid=(m // tm,),
        in_specs=[
            pl.BlockSpec((tm, SSM_W), lambda i: (i, 0)),
            pl.BlockSpec((tm, SGU_W), lambda i: (i, 0)),
            pl.BlockSpec((tm, ATT_W), lambda i: (i, 0)),
            const(D_MIX, d),
            pl.BlockSpec((tm, d), lambda i: (i, 0)),
            const(1, d),
        ],
        out_specs=pl.BlockSpec((tm, d), lambda i: (i, 0)),
        out_shape=jax.ShapeDtypeStruct((m, d), F32),
        compiler_params=pltpu.CompilerParams(
            dimension_semantics=("parallel",),
            vmem_limit_bytes=V7X_VMEM_LIMIT_BYTES),
        name="outproj",
    )(y_ssm, y_sgu, y_att, w_out, x2d, final_g)


def _pick(n, prefs):
    for p in prefs:
        if n % p == 0:
            return p
    raise ValueError(f"no supported tile for extent {n}")


def kernel(x, norm_g, w_in, ssm_a_re, ssm_a_im, ssm_log_step, ssm_b_re, ssm_b_im, ssm_c_re, ssm_c_im,
           ssm_d, glu_w, glu_b, sgu_ln_g, sgu_ln_b, sgu_w, sgu_b, lam_q1, lam_k1, lam_q2, lam_k2,
           attn_subln_g, w_out, final_g):
    nbatch, seq, d = x.shape
    depth = norm_g.shape[0]
    assert d == D_MODEL and seq % 512 == 0 and nbatch == 8
    m = nbatch * seq
    tm_in = _pick(m, (1024, 512, 256))
    tn_in = REST_COLS // 4
    tt = _pick(seq, (64, 32))
    tq = 512

    x2d = x.reshape(m, d).astype(F32)
    fg = final_g.reshape(1, d).astype(F32)
    for l in range(depth):
        g = norm_g[l].reshape(1, d).astype(F32)
        w_l = w_in[l].astype(BF16)
        lam_init = 0.8 - 0.6 * math.exp(-0.3 * l)

        z_rest = _inproj(x2d, g, w_l[:, SSM_COLS:], tm=tm_in, tn=tn_in)

        ssm_p = _ssm_params(ssm_a_re[l], ssm_a_im[l], ssm_log_step[l], ssm_b_re[l], ssm_b_im[l],
                            ssm_c_re[l], ssm_c_im[l], nbatch)
        x_tb = x2d.reshape(nbatch, seq, d).swapaxes(0, 1).reshape(m, d)
        y_ssm_tb = _ssm_branch(x_tb, g, w_l[:, :SSM_COLS], *ssm_p,
                               ssm_d[l].reshape(1, SSM_W).astype(F32), glu_w[l].astype(BF16),
                               glu_b[l].reshape(1, SSM_W).astype(F32), tt=tt, nb=nbatch)
        y_ssm = y_ssm_tb.reshape(seq, nbatch, SSM_W).swapaxes(0, 1).reshape(m, SSM_W)

        y_sgu = _sgu_branch(z_rest, sgu_ln_g[l].reshape(1, SGU_W).astype(F32),
                            sgu_ln_b[l].reshape(1, SGU_W).astype(F32), sgu_w[l].astype(F32),
                            sgu_b[l].astype(F32).T, tm=512)

        y_att = _attention(z_rest, lam_q1[l].reshape(1, -1).astype(F32), lam_k1[l].reshape(1, -1).astype(F32),
                           lam_q2[l].reshape(1, -1).astype(F32), lam_k2[l].reshape(1, -1).astype(F32),
                           attn_subln_g[l].reshape(1, ATT_HEAD_W).astype(F32),
                           nbatch=nbatch, seq=seq, tq=tq, lam_init=lam_init)

        x2d = _outproj(y_ssm, y_sgu, y_att, w_out[l].astype(BF16), x2d, fg,
                       tm=512, final_norm=(l == depth - 1))
    return x2d.reshape(nbatch, seq, d).astype(x.dtype)
```

```python
import functools
import math

import jax
import jax.numpy as jnp
from jax import lax
from jax.experimental import pallas as pl
from jax.experimental.pallas import tpu as pltpu

F32 = jnp.float32
BF16 = jnp.bfloat16

D_MODEL = 1024
D_MIX = 2 * D_MODEL
SSM_W = D_MIX // 4
SSM_GROUP_CH = 16
SSM_GROUPS = SSM_W // SSM_GROUP_CH
SSM_STATE = 64
SSM_CH = SSM_GROUPS * SSM_STATE
SSM_HALF_IN = SSM_W // 2
SSM_HALF_CH = SSM_CH // 2
SGU_W = D_MIX // 4
CHUNK = 128
SGU_HEADS = 4
SGU_HEAD_W = SGU_W // SGU_HEADS
ATT_W = D_MIX // 2
ATT_HEAD_D = 64
ATT_HEADS = ATT_W // (2 * ATT_HEAD_D)
ATT_HEAD_W = 2 * ATT_HEAD_D
ATT_PAD_ROWS = 16
NORM_EPS = 1e-6
LN_EPS = 1e-5

SSM_COLS = 2 * SSM_W
REST_COLS = 3 * SGU_W + 4 * ATT_W
SGU_U_BLK, SGU_V_BLK, SGU_G_BLK = 0, 1, 2
ATT_Q_OFF = 3 * SGU_W // ATT_HEAD_W
ATT_K_OFF = ATT_Q_OFF + ATT_HEADS
ATT_V_OFF = ATT_K_OFF + ATT_HEADS
ATT_G_OFF = ATT_V_OFF + ATT_HEADS

V7X_VMEM_LIMIT_BYTES = 56 * 1024 * 1024
MASK_VALUE = -1e30


def _rms(x, g):
    ms = jnp.mean(x * x, axis=-1, keepdims=True)
    return x * lax.rsqrt(ms + NORM_EPS) * g


def _silu(x):
    return x * (1.0 / (1.0 + jnp.exp(-x)))


def _sigmoid(x):
    return 1.0 / (1.0 + jnp.exp(-x))


def _gelu_exact(x):
    return 0.5 * x * (1.0 + lax.erf(x * math.sqrt(0.5)))


def _inproj_kernel(x_ref, g_ref, w_ref, o_ref, h_ref):
    @pl.when(pl.program_id(1) == 0)
    def _():
        h_ref[...] = _rms(x_ref[...], g_ref[...]).astype(BF16)

    o_ref[...] = jnp.dot(h_ref[...], w_ref[...], preferred_element_type=F32).astype(o_ref.dtype)


def _inproj(x2d, g, w, *, tm, tn):
    m, d = x2d.shape
    n = w.shape[1]
    return pl.pallas_call(
        _inproj_kernel,
        grid=(m // tm, n // tn),
        in_specs=[
            pl.BlockSpec((tm, d), lambda i, j: (i, 0)),
            pl.BlockSpec((1, d), lambda i, j: (0, 0)),
            pl.BlockSpec((d, tn), lambda i, j: (0, j)),
        ],
        out_specs=pl.BlockSpec((tm, tn), lambda i, j: (i, j)),
        out_shape=jax.ShapeDtypeStruct((m, n), BF16),
        scratch_shapes=[pltpu.VMEM((tm, d), BF16)],
        compiler_params=pltpu.CompilerParams(
            dimension_semantics=("parallel", "arbitrary"),
            vmem_limit_bytes=V7X_VMEM_LIMIT_BYTES),
        name="inproj",
    )(x2d, g, w)


def _ssm_kernel(x_ref, g_ref, w_ref, are_ref, aim_ref, bbd_ref, cre_ref, ncim_ref, d_ref,
                gw_ref, gb_ref, o_ref, bre_ref, bim_ref, carry_ref, *, tt, nb):
    @pl.when(pl.program_id(0) == 0)
    def _():
        carry_ref[...] = jnp.zeros_like(carry_ref)

    h = _rms(x_ref[...], g_ref[...]).astype(BF16)
    z = jnp.dot(h, w_ref[...], preferred_element_type=F32)
    u = z[:, :SSM_W]
    gate = z[:, SSM_W:]
    ub = u.astype(BF16)

    for hf in range(2):
        r = jnp.dot(ub[:, hf * SSM_HALF_IN:(hf + 1) * SSM_HALF_IN], bbd_ref[hf],
                    preferred_element_type=F32)
        bre_ref[:, hf * SSM_HALF_CH:(hf + 1) * SSM_HALF_CH] = r[:, :SSM_HALF_CH]
        bim_ref[:, hf * SSM_HALF_CH:(hf + 1) * SSM_HALF_CH] = r[:, SSM_HALF_CH:]

    def step(t, carry):
        sr, si = carry
        rows = pl.ds(pl.multiple_of(t * nb, nb), nb)
        ar = are_ref[...]
        ai = aim_ref[...]
        nsr = ar * sr - ai * si + bre_ref[rows, :]
        nsi = ar * si + ai * sr + bim_ref[rows, :]
        bre_ref[rows, :] = nsr
        bim_ref[rows, :] = nsi
        return nsr, nsi

    sr, si = lax.fori_loop(0, tt, step, (carry_ref[0], carry_ref[1]))
    carry_ref[0] = sr
    carry_ref[1] = si

    ys = []
    for hf in range(2):
        cols = slice(hf * SSM_HALF_CH, (hf + 1) * SSM_HALF_CH)
        yh = jnp.dot(bre_ref[:, cols].astype(BF16), cre_ref[hf], preferred_element_type=F32)
        yh = yh + jnp.dot(bim_ref[:, cols].astype(BF16), ncim_ref[hf], preferred_element_type=F32)
        ys.append(yh)
    y = jnp.concatenate(ys, axis=-1) + d_ref[...] * u
    y = _gelu_exact(y)
    y = y * _sigmoid(jnp.dot(y.astype(BF16), gw_ref[...], preferred_element_type=F32) + gb_ref[...])
    o_ref[...] = (y * _silu(gate)).astype(o_ref.dtype)


def _ssm_branch(x_tb, g, w_ssm, are8, aim8, bbd, cre_bd, ncim_bd, d, glu_w, glu_b, *, tt, nb):
    m, dm = x_tb.shape
    r = tt * nb
    const = lambda *shape: pl.BlockSpec(shape, lambda i: (0,) * len(shape))
    return pl.pallas_call(
        functools.partial(_ssm_kernel, tt=tt, nb=nb),
        grid=(m // r,),
        in_specs=[
            pl.BlockSpec((r, dm), lambda i: (i, 0)),
            const(1, dm),
            const(dm, SSM_COLS),
            const(nb, SSM_CH),
            const(nb, SSM_CH),
            const(2, SSM_HALF_IN, 2 * SSM_HALF_CH),
            const(2, SSM_HALF_CH, SSM_HALF_IN),
            const(2, SSM_HALF_CH, SSM_HALF_IN),
            const(1, SSM_W),
            const(SSM_W, SSM_W),
            const(1, SSM_W),
        ],
        out_specs=pl.BlockSpec((r, SSM_W), lambda i: (i, 0)),
        out_shape=jax.ShapeDtypeStruct((m, SSM_W), BF16),
        scratch_shapes=[
            pltpu.VMEM((r, SSM_CH), F32),
            pltpu.VMEM((r, SSM_CH), F32),
            pltpu.VMEM((2, nb, SSM_CH), F32),
        ],
        compiler_params=pltpu.CompilerParams(
            dimension_semantics=("arbitrary",),
            vmem_limit_bytes=V7X_VMEM_LIMIT_BYTES),
        name="ssm_branch",
    )(x_tb, g, w_ssm, are8, aim8, bbd, cre_bd, ncim_bd, d, glu_w, glu_b)


def _ssm_params(a_re, a_im, log_step, b_re, b_im, c_re, c_im, nb):
    lr = a_re.astype(F32)
    li = a_im.astype(F32)
    step = jnp.exp(log_step.astype(F32))[:, None]
    mag = jnp.exp(step * lr)
    ang = step * li
    ab_re = mag * jnp.cos(ang)
    ab_im = mag * jnp.sin(ang)
    den = lr * lr + li * li
    nr = ab_re - 1.0
    ni = ab_im
    co_re = (nr * lr + ni * li) / den
    co_im = (ni * lr - nr * li) / den
    br = b_re.astype(F32)
    bi = b_im.astype(F32)
    bb_re = co_re[..., None] * br - co_im[..., None] * bi
    bb_im = co_re[..., None] * bi + co_im[..., None] * br
    gh = SSM_GROUPS // 2
    eye = jnp.eye(gh, dtype=F32)

    def b_blockdiag(bb):
        t = bb.reshape(2, gh, SSM_STATE, SSM_GROUP_CH)
        t = jnp.einsum('fgph,gk->fghkp', t, eye)
        return t.reshape(2, gh * SSM_GROUP_CH, gh * SSM_STATE)

    def c_blockdiag(c):
        t = c.astype(F32).reshape(2, gh, SSM_GROUP_CH, SSM_STATE)
        t = jnp.einsum('fghp,gk->fgpkh', t, eye)
        return t.reshape(2, gh * SSM_STATE, gh * SSM_GROUP_CH)

    bbd = jnp.concatenate([b_blockdiag(bb_re), b_blockdiag(bb_im)], axis=-1).astype(BF16)
    cre_bd = c_blockdiag(c_re).astype(BF16)
    ncim_bd = (-c_blockdiag(c_im)).astype(BF16)
    are8 = jnp.broadcast_to(ab_re.reshape(1, SSM_CH), (nb, SSM_CH))
    aim8 = jnp.broadcast_to(ab_im.reshape(1, SSM_CH), (nb, SSM_CH))
    return are8, aim8, bbd, cre_bd, ncim_bd


def _sgu_kernel(u_ref, v_ref, gt_ref, lg_ref, lb_ref, w_ref, bt_ref, o_ref, *, tm):
    v = v_ref[...].astype(F32)
    mu = jnp.mean(v, axis=-1, keepdims=True)
    vc = v - mu
    var = jnp.mean(vc * vc, axis=-1, keepdims=True)
    vn = (vc * lax.rsqrt(var + LN_EPS) * lg_ref[...] + lb_ref[...]).astype(BF16)
    row = lax.broadcasted_iota(jnp.int32, (CHUNK, CHUNK), 0)
    col = lax.broadcasted_iota(jnp.int32, (CHUNK, CHUNK), 1)
    for hd in range(SGU_HEADS):
        w = jnp.where(col <= row, w_ref[hd], 0.0).astype(BF16)
        bias = bt_ref[:, hd:hd + 1]
        cs = slice(hd * SGU_HEAD_W, (hd + 1) * SGU_HEAD_W)
        for c in range(tm // CHUNK):
            rs = slice(c * CHUNK, (c + 1) * CHUNK)
            s = jnp.dot(w, vn[rs, cs], preferred_element_type=F32) + bias
            uu = u_ref[rs, cs].astype(F32)
            gg = gt_ref[rs, cs].astype(F32)
            o_ref[rs, cs] = (uu * s * _silu(gg)).astype(o_ref.dtype)


def _sgu_branch(z_rest, ln_g, ln_b, w_s, b_t, *, tm):
    m = z_rest.shape[0]
    const = lambda *shape: pl.BlockSpec(shape, lambda i: (0,) * len(shape))
    return pl.pallas_call(
        functools.partial(_sgu_kernel, tm=tm),
        grid=(m // tm,),
        in_specs=[
            pl.BlockSpec((tm, SGU_W), lambda i: (i, SGU_U_BLK)),
            pl.BlockSpec((tm, SGU_W), lambda i: (i, SGU_V_BLK)),
            pl.BlockSpec((tm, SGU_W), lambda i: (i, SGU_G_BLK)),
            const(1, SGU_W),
            const(1, SGU_W),
            const(SGU_HEADS, CHUNK, CHUNK),
            const(CHUNK, SGU_HEADS),
        ],
        out_specs=pl.BlockSpec((tm, SGU_W), lambda i: (i, 0)),
        out_shape=jax.ShapeDtypeStruct((m, SGU_W), BF16),
        compiler_params=pltpu.CompilerParams(dimension_semantics=("parallel",)),
        name="sgu_branch",
    )(z_rest, z_rest, z_rest, ln_g, ln_b, w_s, b_t)


def _attn_kernel(q_ref, k_ref, v_ref, gt_ref, lq1_ref, lk1_ref, lq2_ref, lk2_ref, sg_ref,
                 o_ref, qzt_ref, vt_ref, s_ref, m_ref, acc_ref, *, tq, lam_init):
    qi = pl.program_id(2)
    nkb = vt_ref.shape[0]

    @pl.when(qi == 0)
    def _():
        pad = lax.broadcasted_iota(jnp.int32, (ATT_PAD_ROWS, tq), 0)
        ones_row = jnp.where(pad == 0, 1.0, 0.0).astype(BF16)
        for j in range(nkb):
            vt_ref[j, :ATT_HEAD_W, :] = v_ref[j * tq:(j + 1) * tq, :].T
            vt_ref[j, ATT_HEAD_W:, :] = ones_row

    qt = (q_ref[...] * ATT_HEAD_D ** -0.5).T
    sub = lax.broadcasted_iota(jnp.int32, (ATT_HEAD_W, tq), 0)
    qzt_ref[:, :tq] = jnp.where(sub < ATT_HEAD_D, qt, 0).astype(BF16)
    qzt_ref[:, tq:] = jnp.where(sub >= ATT_HEAD_D, qt, 0).astype(BF16)

    def scores(slot, kb):
        rows = pl.ds(pl.multiple_of(kb * tq, tq), tq)
        s_ref[slot] = jnp.dot(k_ref[rows, :], qzt_ref[...], preferred_element_type=F32)

    def softmax_pv(slot, kb, masked):
        s = s_ref[slot]
        if masked:
            key = lax.broadcasted_iota(jnp.int32, (tq, 2 * tq), 0)
            qry = lax.broadcasted_iota(jnp.int32, (tq, 2 * tq), 1)
            qry = jnp.where(qry >= tq, qry - tq, qry)
            s = jnp.where(key <= qry, s, MASK_VALUE)
        m = m_ref[...]
        m_new = jnp.maximum(m, jnp.max(s, axis=0, keepdims=True))
        alpha = jnp.exp(m - m_new)
        p = jnp.exp(s - m_new).astype(BF16)
        acc_ref[...] = alpha * acc_ref[...] + jnp.dot(vt_ref[kb], p, preferred_element_type=F32)
        m_ref[...] = m_new

    m_ref[...] = jnp.full(m_ref.shape, MASK_VALUE, F32)
    acc_ref[...] = jnp.zeros(acc_ref.shape, F32)

    scores(0, 0)

    def pair(j, c):
        scores(1, 2 * j + 1)
        softmax_pv(0, 2 * j, False)
        scores(0, 2 * j + 2)
        softmax_pv(1, 2 * j + 1, False)
        return c

    lax.fori_loop(0, qi // 2, pair, 0)

    @pl.when(qi % 2 == 1)
    def _():
        scores(1, qi)
        softmax_pv(0, qi - 1, False)
        softmax_pv(1, qi, True)

    @pl.when(qi % 2 == 0)
    def _():
        softmax_pv(0, qi, True)

    l = acc_ref[ATT_HEAD_W:ATT_HEAD_W + 1, :]
    acc = acc_ref[:ATT_HEAD_W, :]

    lam = (jnp.exp(jnp.sum(lq1_ref[...] * lk1_ref[...], axis=-1, keepdims=True))
           - jnp.exp(jnp.sum(lq2_ref[...] * lk2_ref[...], axis=-1, keepdims=True)) + lam_init)
    ot = acc[:, :tq] / l[:, :tq] - lam * (acc[:, tq:] / l[:, tq:])
    o = _rms(ot.T, sg_ref[...]) * (1.0 - lam_init)
    o_ref[...] = (o * _silu(gt_ref[...].astype(F32))).astype(o_ref.dtype)


def _attention(z_rest, lq1, lk1, lq2, lk2, subln_g, *, nbatch, seq, tq, lam_init):
    nq = seq // tq
    const = lambda *shape: pl.BlockSpec(shape, lambda b, h, i: (0,) * len(shape))
    return pl.pallas_call(
        functools.partial(_attn_kernel, tq=tq, lam_init=lam_init),
        grid=(nbatch, ATT_HEADS, nq),
        in_specs=[
            pl.BlockSpec((tq, ATT_HEAD_W), lambda b, h, i: (b * nq + i, ATT_Q_OFF + h)),
            pl.BlockSpec((seq, ATT_HEAD_W), lambda b, h, i: (b, ATT_K_OFF + h)),
            pl.BlockSpec((seq, ATT_HEAD_W), lambda b, h, i: (b, ATT_V_OFF + h)),
            pl.BlockSpec((tq, ATT_HEAD_W), lambda b, h, i: (b * nq + i, ATT_G_OFF + h)),
            const(1, ATT_HEAD_D),
            const(1, ATT_HEAD_D),
            const(1, ATT_HEAD_D),
            const(1, ATT_HEAD_D),
            const(1, ATT_HEAD_W),
        ],
        out_specs=pl.BlockSpec((tq, ATT_HEAD_W), lambda b, h, i: (b * nq + i, h)),
        out_shape=jax.ShapeDtypeStruct((nbatch * seq, ATT_W), BF16),
        scratch_shapes=[pltpu.VMEM((ATT_HEAD_W, 2 * tq), BF16),
                        pltpu.VMEM((nq, ATT_HEAD_W + ATT_PAD_ROWS, tq), BF16),
                        pltpu.VMEM((2, tq, 2 * tq), F32),
                        pltpu.VMEM((1, 2 * tq), F32),
                        pltpu.VMEM((ATT_HEAD_W + ATT_PAD_ROWS, 2 * tq), F32)],
        compiler_params=pltpu.CompilerParams(
            dimension_semantics=("parallel", "parallel", "arbitrary"),
            vmem_limit_bytes=V7X_VMEM_LIMIT_BYTES),
        name="diff_attention",
    )(z_rest, z_rest, z_rest, z_rest, lq1, lk1, lq2, lk2, subln_g)


def _outproj_kernel(ys_ref, yg_ref, ya_ref, w_ref, x_ref, fg_ref, o_ref, *, final_norm):
    acc = jnp.dot(ys_ref[...], w_ref[:SSM_W, :], preferred_element_type=F32)
    acc = acc + jnp.dot(yg_ref[...], w_ref[SSM_W:SSM_W + SGU_W, :], preferred_element_type=F32)
    acc = acc + jnp.dot(ya_ref[...], w_ref[SSM_W + SGU_W:, :], preferred_element_type=F32)
    xn = x_ref[...] + acc
    if final_norm:
        xn = _rms(xn, fg_ref[...])
    o_ref[...] = xn


def _outproj(y_ssm, y_sgu, y_att, w_out, x2d, final_g, *, tm, final_norm):
    m, d = x2d.shape
    const = lambda *shape: pl.BlockSpec(shape, lambda i: (0,) * len(shape))
    return pl.pallas_call(
        functools.partial(_outproj_kernel, final_norm=final_norm),
        grid=(m // tm,),
        in_specs=[
            pl.BlockSpec((tm, SSM_W), lambda i: (i, 0)),
            pl.BlockSpec((tm, SGU_W), lambda i: (i, 0)),
            pl.BlockSpec((tm, ATT_W), lambda i: (i, 0)),
            const(D_MIX, d),
            pl.BlockSpec((tm, d), lambda i: (i, 0)),
            const(1, d),
        ],
        out_specs=pl.BlockSpec((tm, d), lambda i: (i, 0)),
        out_shape=jax.ShapeDtypeStruct((m, d), F32),
        compiler_params=pltpu.CompilerParams(
            dimension_semantics=("parallel",),
            vmem_limit_bytes=V7X_VMEM_LIMIT_BYTES),
        name="outproj",
    )(y_ssm, y_sgu, y_att, w_out, x2d, final_g)


def _pick(n, prefs):
    for p in prefs:
        if n % p == 0:
            return p
    raise ValueError(f"no supported tile for extent {n}")


def kernel(x, norm_g, w_in, ssm_a_re, ssm_a_im, ssm_log_step, ssm_b_re, ssm_b_im, ssm_c_re, ssm_c_im,
           ssm_d, glu_w, glu_b, sgu_ln_g, sgu_ln_b, sgu_w, sgu_b, lam_q1, lam_k1, lam_q2, lam_k2,
           attn_subln_g, w_out, final_g):
    nbatch, seq, d = x.shape
    depth = norm_g.shape[0]
    assert d == D_MODEL and seq % 512 == 0 and nbatch == 8
    m = nbatch * seq
    tm_in = _pick(m, (1024, 512, 256))
    tn_in = REST_COLS // 4
    tt = _pick(seq, (64, 32))
    tq = 512

    x2d = x.reshape(m, d).astype(F32)
    fg = final_g.reshape(1, d).astype(F32)
    for l in range(depth):
        g = norm_g[l].reshape(1, d).astype(F32)
        w_l = w_in[l].astype(BF16)
        lam_init = 0.8 - 0.6 * math.exp(-0.3 * l)

        z_rest = _inproj(x2d, g, w_l[:, SSM_COLS:], tm=tm_in, tn=tn_in)

        ssm_p = _ssm_params(ssm_a_re[l], ssm_a_im[l], ssm_log_step[l], ssm_b_re[l], ssm_b_im[l],
                            ssm_c_re[l], ssm_c_im[l], nbatch)
        x_tb = x2d.reshape(nbatch, seq, d).swapaxes(0, 1).reshape(m, d)
        y_ssm_tb = _ssm_branch(x_tb, g, w_l[:, :SSM_COLS], *ssm_p,
                               ssm_d[l].reshape(1, SSM_W).astype(F32), glu_w[l].astype(BF16),
                               glu_b[l].reshape(1, SSM_W).astype(F32), tt=tt, nb=nbatch)
        y_ssm = y_ssm_tb.reshape(seq, nbatch, SSM_W).swapaxes(0, 1).reshape(m, SSM_W)

        y_sgu = _sgu_branch(z_rest, sgu_ln_g[l].reshape(1, SGU_W).astype(F32),
                            sgu_ln_b[l].reshape(1, SGU_W).astype(F32), sgu_w[l].astype(F32),
                            sgu_b[l].astype(F32).T, tm=512)

        y_att = _attention(z_rest, lam_q1[l].reshape(1, -1).astype(F32), lam_k1[l].reshape(1, -1).astype(F32),
                           lam_q2[l].reshape(1, -1).astype(F32), lam_k2[l].reshape(1, -1).astype(F32),
                           attn_subln_g[l].reshape(1, ATT_HEAD_W).astype(F32),
                           nbatch=nbatch, seq=seq, tq=tq, lam_init=lam_init)

        x2d = _outproj(y_ssm, y_sgu, y_att, w_out[l].astype(BF16), x2d, fg,
                       tm=512, final_norm=(l == depth - 1))
    return x2d.reshape(nbatch, seq, d).astype(x.dtype)
```

```python
import functools
import math

import jax
import jax.numpy as jnp
from jax import lax
from jax.experimental import pallas as pl
from jax.experimental.pallas import tpu as pltpu

F32 = jnp.float32
BF16 = jnp.bfloat16

D_MODEL = 1024
D_MIX = 2 * D_MODEL
SSM_W = D_MIX // 4
SSM_GROUP_CH = 16
SSM_GROUPS = SSM_W // SSM_GROUP_CH
SSM_STATE = 64
SSM_CH = SSM_GROUPS * SSM_STATE
SSM_HALF_IN = SSM_W // 2
SSM_HALF_CH = SSM_CH // 2
SGU_W = D_MIX // 4
CHUNK = 128
SGU_HEADS = 4
SGU_HEAD_W = SGU_W // SGU_HEADS
ATT_W = D_MIX // 2
ATT_HEAD_D = 64
ATT_HEADS = ATT_W // (2 * ATT_HEAD_D)
ATT_HEAD_W = 2 * ATT_HEAD_D
ATT_PAD_ROWS = 16
NORM_EPS = 1e-6
LN_EPS = 1e-5

SSM_COLS = 2 * SSM_W
REST_COLS = 3 * SGU_W + 4 * ATT_W
SGU_U_BLK, SGU_V_BLK, SGU_G_BLK = 0, 1, 2
ATT_Q_OFF = 3 * SGU_W // ATT_HEAD_W
ATT_K_OFF = ATT_Q_OFF + ATT_HEADS
ATT_V_OFF = ATT_K_OFF + ATT_HEADS
ATT_G_OFF = ATT_V_OFF + ATT_HEADS

V7X_VMEM_LIMIT_BYTES = 56 * 1024 * 1024
MXU_COL_CHUNK = 1024
LOG2_E = math.log2(math.e)
MASK_VALUE = -1e30


def _rms(x, g):
    ms = jnp.mean(x * x, axis=-1, keepdims=True)
    return x * lax.rsqrt(ms + NORM_EPS) * g


def _silu(x):
    return x * (1.0 / (1.0 + jnp.exp(-x)))


def _sigmoid(x):
    return 1.0 / (1.0 + jnp.exp(-x))


def _gelu_exact(x):
    return 0.5 * x * (1.0 + lax.erf(x * math.sqrt(0.5)))


def _inproj_kernel(x_ref, g_ref, w_ref, o_ref, *, col_chunks):
    h = _rms(x_ref[...], g_ref[...]).astype(BF16)
    c0 = 0
    for cw in col_chunks:
        o_ref[:, c0:c0 + cw] = jnp.dot(h, w_ref[:, c0:c0 + cw],
                                       preferred_element_type=F32).astype(o_ref.dtype)
        c0 += cw


def _resident(*shape):
    return pl.BlockSpec(shape, lambda *_: (0,) * len(shape), pipeline_mode=pl.Buffered(1))


def _inproj(x2d, g, w, *, tm):
    m, d = x2d.shape
    n = w.shape[1]
    col_chunks = (MXU_COL_CHUNK,) * (n // MXU_COL_CHUNK)
    if n % MXU_COL_CHUNK:
        col_chunks += (n % MXU_COL_CHUNK,)
    return pl.pallas_call(
        functools.partial(_inproj_kernel, col_chunks=col_chunks),
        grid=(m // tm,),
        in_specs=[
            pl.BlockSpec((tm, d), lambda i: (i, 0)),
            _resident(1, d),
            _resident(d, n),
        ],
        out_specs=pl.BlockSpec((tm, n), lambda i: (i, 0)),
        out_shape=jax.ShapeDtypeStruct((m, n), BF16),
        compiler_params=pltpu.CompilerParams(
            dimension_semantics=("parallel",),
            vmem_limit_bytes=V7X_VMEM_LIMIT_BYTES),
        name="inproj",
    )(x2d, g, w)


def _ssm_kernel(x_ref, g_ref, w_ref, are_ref, aim_ref, bbd_ref, cre_ref, ncim_ref, d_ref,
                gw_ref, gb_ref, o_ref, bre_ref, bim_ref, carry_ref, *, tt, nb):
    @pl.when(pl.program_id(0) == 0)
    def _():
        carry_ref[...] = jnp.zeros_like(carry_ref)

    h = _rms(x_ref[...], g_ref[...]).astype(BF16)
    z = jnp.dot(h, w_ref[...], preferred_element_type=F32)
    u = z[:, :SSM_W]
    gate = z[:, SSM_W:]
    ub = u.astype(BF16)

    for hf in range(2):
        r = jnp.dot(ub[:, hf * SSM_HALF_IN:(hf + 1) * SSM_HALF_IN], bbd_ref[hf],
                    preferred_element_type=F32)
        bre_ref[:, hf * SSM_HALF_CH:(hf + 1) * SSM_HALF_CH] = r[:, :SSM_HALF_CH]
        bim_ref[:, hf * SSM_HALF_CH:(hf + 1) * SSM_HALF_CH] = r[:, SSM_HALF_CH:]

    def step(t, carry):
        sr, si = carry
        rows = pl.ds(pl.multiple_of(t * nb, nb), nb)
        ar = are_ref[...]
        ai = aim_ref[...]
        nsr = ar * sr - ai * si + bre_ref[rows, :]
        nsi = ar * si + ai * sr + bim_ref[rows, :]
        bre_ref[rows, :] = nsr
        bim_ref[rows, :] = nsi
        return nsr, nsi

    sr, si = lax.fori_loop(0, tt, step, (carry_ref[0], carry_ref[1]))
    carry_ref[0] = sr
    carry_ref[1] = si

    ys = []
    for hf in range(2):
        cols = slice(hf * SSM_HALF_CH, (hf + 1) * SSM_HALF_CH)
        yh = jnp.dot(bre_ref[:, cols].astype(BF16), cre_ref[hf], preferred_element_type=F32)
        yh = yh + jnp.dot(bim_ref[:, cols].astype(BF16), ncim_ref[hf], preferred_element_type=F32)
        ys.append(yh)
    y = jnp.concatenate(ys, axis=-1) + d_ref[...] * u
    y = _gelu_exact(y)
    y = y * _sigmoid(jnp.dot(y.astype(BF16), gw_ref[...], preferred_element_type=F32) + gb_ref[...])
    o_ref[...] = (y * _silu(gate)).astype(o_ref.dtype)


def _ssm_branch(x_tb, g, w_ssm, are8, aim8, bbd, cre_bd, ncim_bd, d, glu_w, glu_b, *, tt, nb):
    m, dm = x_tb.shape
    r = tt * nb
    const = lambda *shape: pl.BlockSpec(shape, lambda i: (0,) * len(shape))
    return pl.pallas_call(
        functools.partial(_ssm_kernel, tt=tt, nb=nb),
        grid=(m // r,),
        in_specs=[
            pl.BlockSpec((r, dm), lambda i: (i, 0)),
            const(1, dm),
            const(dm, SSM_COLS),
            const(nb, SSM_CH),
            const(nb, SSM_CH),
            const(2, SSM_HALF_IN, 2 * SSM_HALF_CH),
            const(2, SSM_HALF_CH, SSM_HALF_IN),
            const(2, SSM_HALF_CH, SSM_HALF_IN),
            const(1, SSM_W),
            const(SSM_W, SSM_W),
            const(1, SSM_W),
        ],
        out_specs=pl.BlockSpec((r, SSM_W), lambda i: (i, 0)),
        out_shape=jax.ShapeDtypeStruct((m, SSM_W), BF16),
        scratch_shapes=[
            pltpu.VMEM((r, SSM_CH), F32),
            pltpu.VMEM((r, SSM_CH), F32),
            pltpu.VMEM((2, nb, SSM_CH), F32),
        ],
        compiler_params=pltpu.CompilerParams(
            dimension_semantics=("arbitrary",),
            vmem_limit_bytes=V7X_VMEM_LIMIT_BYTES),
        name="ssm_branch",
    )(x_tb, g, w_ssm, are8, aim8, bbd, cre_bd, ncim_bd, d, glu_w, glu_b)


def _ssm_params(a_re, a_im, log_step, b_re, b_im, c_re, c_im, nb):
    lr = a_re.astype(F32)
    li = a_im.astype(F32)
    step = jnp.exp(log_step.astype(F32))[:, None]
    mag = jnp.exp(step * lr)
    ang = step * li
    ab_re = mag * jnp.cos(ang)
    ab_im = mag * jnp.sin(ang)
    den = lr * lr + li * li
    nr = ab_re - 1.0
    ni = ab_im
    co_re = (nr * lr + ni * li) / den
    co_im = (ni * lr - nr * li) / den
    br = b_re.astype(F32)
    bi = b_im.astype(F32)
    bb_re = co_re[..., None] * br - co_im[..., None] * bi
    bb_im = co_re[..., None] * bi + co_im[..., None] * br
    gh = SSM_GROUPS // 2
    eye = jnp.eye(gh, dtype=F32)

    def b_blockdiag(bb):
        t = bb.reshape(2, gh, SSM_STATE, SSM_GROUP_CH)
        t = jnp.einsum('fgph,gk->fghkp', t, eye)
        return t.reshape(2, gh * SSM_GROUP_CH, gh * SSM_STATE)

    def c_blockdiag(c):
        t = c.astype(F32).reshape(2, gh, SSM_GROUP_CH, SSM_STATE)
        t = jnp.einsum('fghp,gk->fgpkh', t, eye)
        return t.reshape(2, gh * SSM_STATE, gh * SSM_GROUP_CH)

    bbd = jnp.concatenate([b_blockdiag(bb_re), b_blockdiag(bb_im)], axis=-1).astype(BF16)
    cre_bd = c_blockdiag(c_re).astype(BF16)
    ncim_bd = (-c_blockdiag(c_im)).astype(BF16)
    are8 = jnp.broadcast_to(ab_re.reshape(1, SSM_CH), (nb, SSM_CH))
    aim8 = jnp.broadcast_to(ab_im.reshape(1, SSM_CH), (nb, SSM_CH))
    return are8, aim8, bbd, cre_bd, ncim_bd


def _sgu_mix(u_ref, v_ref, gt_ref, lg_ref, lb_ref, w_ref, bt_ref, o_ref, tm):
    v = v_ref[...].astype(F32)
    mu = jnp.mean(v, axis=-1, keepdims=True)
    vc = v - mu
    var = jnp.mean(vc * vc, axis=-1, keepdims=True)
    vn = (vc * lax.rsqrt(var + LN_EPS) * lg_ref[...] + lb_ref[...]).astype(BF16)
    row = lax.broadcasted_iota(jnp.int32, (CHUNK, CHUNK), 0)
    col = lax.broadcasted_iota(jnp.int32, (CHUNK, CHUNK), 1)
    for hd in range(SGU_HEADS):
        w = jnp.where(col <= row, w_ref[hd], 0.0).astype(BF16)
        bias = bt_ref[:, hd:hd + 1]
        cs = slice(hd * SGU_HEAD_W, (hd + 1) * SGU_HEAD_W)
        for c in range(tm // CHUNK):
            rs = slice(c * CHUNK, (c + 1) * CHUNK)
            s = jnp.dot(w, vn[rs, cs], preferred_element_type=F32) + bias
            uu = u_ref[rs, cs].astype(F32)
            gg = gt_ref[rs, cs].astype(F32)
            o_ref[rs, cs] = (uu * s * _silu(gg)).astype(o_ref.dtype)


def _attn_kernel(q_ref, k_ref, v_ref, gt_ref, lq1_ref, lk1_ref, lq2_ref, lk2_ref, sg_ref,
                 o_ref, qzt_ref, vt_ref, s_ref, m_ref, acc_ref, *, tq, lam_init):
    qi = pl.program_id(2)
    nkb = vt_ref.shape[0]

    @pl.when(qi == 0)
    def _():
        pad = lax.broadcasted_iota(jnp.int32, (ATT_PAD_ROWS, tq), 0)
        ones_row = jnp.where(pad == 0, 1.0, 0.0).astype(BF16)
        for j in range(nkb):
            vt_ref[j, :ATT_HEAD_W, :] = v_ref[j * tq:(j + 1) * tq, :].T
            vt_ref[j, ATT_HEAD_W:, :] = ones_row

    qt = q_ref[...].T
    sub = lax.broadcasted_iota(jnp.int32, (ATT_HEAD_W, tq), 0)
    qzt_ref[:, :tq] = jnp.where(sub < ATT_HEAD_D, qt, 0).astype(BF16)
    qzt_ref[:, tq:] = jnp.where(sub >= ATT_HEAD_D, qt, 0).astype(BF16)

    def scores(slot, kb):
        rows = pl.ds(pl.multiple_of(kb * tq, tq), tq)
        s_ref[slot] = jnp.dot(k_ref[rows, :], qzt_ref[...], preferred_element_type=F32)

    def softmax_pv(slot, kb, masked):
        s = s_ref[slot]
        if masked:
            key = lax.broadcasted_iota(jnp.int32, (tq, 2 * tq), 0)
            qry = lax.broadcasted_iota(jnp.int32, (tq, 2 * tq), 1)
            qry = jnp.where(qry >= tq, qry - tq, qry)
            s = jnp.where(key <= qry, s, MASK_VALUE)
        m = m_ref[...]
        m_new = jnp.maximum(m, jnp.max(s, axis=0, keepdims=True))
        alpha = jnp.exp2(m - m_new)
        p = jnp.exp2(s - m_new).astype(BF16)
        acc_ref[...] = alpha * acc_ref[...] + jnp.dot(vt_ref[kb], p, preferred_element_type=F32)
        m_ref[...] = m_new

    m_ref[...] = jnp.full(m_ref.shape, MASK_VALUE, F32)
    acc_ref[...] = jnp.zeros(acc_ref.shape, F32)

    scores(0, 0)

    def pair(j, c):
        scores(1, 2 * j + 1)
        softmax_pv(0, 2 * j, False)
        scores(0, 2 * j + 2)
        softmax_pv(1, 2 * j + 1, False)
        return c

    lax.fori_loop(0, qi // 2, pair, 0)

    @pl.when(qi % 2 == 1)
    def _():
        scores(1, qi)
        softmax_pv(0, qi - 1, False)
        softmax_pv(1, qi, True)

    @pl.when(qi % 2 == 0)
    def _():
        softmax_pv(0, qi, True)

    rl = 1.0 / acc_ref[ATT_HEAD_W:ATT_HEAD_W + 1, :]
    acc = acc_ref[:ATT_HEAD_W, :]

    lam = (jnp.exp(jnp.sum(lq1_ref[...] * lk1_ref[...], axis=-1, keepdims=True))
           - jnp.exp(jnp.sum(lq2_ref[...] * lk2_ref[...], axis=-1, keepdims=True)) + lam_init)
    ot = acc[:, :tq] * rl[:, :tq] - lam * (acc[:, tq:] * rl[:, tq:])
    o = _rms(ot.T, sg_ref[...]) * (1.0 - lam_init)
    o_ref[...] = (o * _silu(gt_ref[...].astype(F32))).astype(o_ref.dtype)


def _attention(z_rest, lq1, lk1, lq2, lk2, subln_g, *, nbatch, seq, tq, lam_init):
    nq = seq // tq
    const = lambda *shape: pl.BlockSpec(shape, lambda b, h, i: (0,) * len(shape))
    return pl.pallas_call(
        functools.partial(_attn_kernel, tq=tq, lam_init=lam_init),
        grid=(nbatch, ATT_HEADS, nq),
        in_specs=[
            pl.BlockSpec((tq, ATT_HEAD_W), lambda b, h, i: (b * nq + i, ATT_Q_OFF + h)),
            pl.BlockSpec((seq, ATT_HEAD_W), lambda b, h, i: (b, ATT_K_OFF + h)),
            pl.BlockSpec((seq, ATT_HEAD_W), lambda b, h, i: (b, ATT_V_OFF + h)),
            pl.BlockSpec((tq, ATT_HEAD_W), lambda b, h, i: (b * nq + i, ATT_G_OFF + h)),
            const(1, ATT_HEAD_D),
            const(1, ATT_HEAD_D),
            const(1, ATT_HEAD_D),
            const(1, ATT_HEAD_D),
            const(1, ATT_HEAD_W),
        ],
        out_specs=pl.BlockSpec((tq, ATT_HEAD_W), lambda b, h, i: (b * nq + i, h)),
        out_shape=jax.ShapeDtypeStruct((nbatch * seq, ATT_W), BF16),
        scratch_shapes=[pltpu.VMEM((ATT_HEAD_W, 2 * tq), BF16),
                        pltpu.VMEM((nq, ATT_HEAD_W + ATT_PAD_ROWS, tq), BF16),
                        pltpu.VMEM((2, tq, 2 * tq), F32),
                        pltpu.VMEM((1, 2 * tq), F32),
                        pltpu.VMEM((ATT_HEAD_W + ATT_PAD_ROWS, 2 * tq), F32)],
        compiler_params=pltpu.CompilerParams(
            dimension_semantics=("parallel", "parallel", "arbitrary"),
            vmem_limit_bytes=V7X_VMEM_LIMIT_BYTES),
        name="diff_attention",
    )(z_rest, z_rest, z_rest, z_rest, lq1, lk1, lq2, lk2, subln_g)


def _outproj_kernel(ys_ref, u_ref, v_ref, gt_ref, lg_ref, lb_ref, ws_ref, bt_ref, ya_ref, w_ref,
                    x_ref, fg_ref, o_ref, yg_ref, *, tm, final_norm):
    _sgu_mix(u_ref, v_ref, gt_ref, lg_ref, lb_ref, ws_ref, bt_ref, yg_ref, tm)
    acc = jnp.dot(ys_ref[...], w_ref[:SSM_W, :], preferred_element_type=F32)
    acc = acc + jnp.dot(yg_ref[...], w_ref[SSM_W:SSM_W + SGU_W, :], preferred_element_type=F32)
    acc = acc + jnp.dot(ya_ref[...], w_ref[SSM_W + SGU_W:, :], preferred_element_type=F32)
    xn = x_ref[...] + acc
    if final_norm:
        xn = _rms(xn, fg_ref[...])
    o_ref[...] = xn


def _outproj(y_ssm, z_rest, sgu_ln_g, sgu_ln_b, sgu_w, sgu_bt, y_att, w_out, x2d, final_g, *,
             tm, final_norm):
    m, d = x2d.shape
    assert tm % CHUNK == 0
    return pl.pallas_call(
        functools.partial(_outproj_kernel, tm=tm, final_norm=final_norm),
        grid=(m // tm,),
        in_specs=[
            pl.BlockSpec((tm, SSM_W), lambda i: (i, 0)),
            pl.BlockSpec((tm, SGU_W), lambda i: (i, SGU_U_BLK)),
            pl.BlockSpec((tm, SGU_W), lambda i: (i, SGU_V_BLK)),
            pl.BlockSpec((tm, SGU_W), lambda i: (i, SGU_G_BLK)),
            _resident(1, SGU_W),
            _resident(1, SGU_W),
            _resident(SGU_HEADS, CHUNK, CHUNK),
            _resident(CHUNK, SGU_HEADS),
            pl.BlockSpec((tm, ATT_W), lambda i: (i, 0)),
            _resident(D_MIX, d),
            pl.BlockSpec((tm, d), lambda i: (i, 0)),
            _resident(1, d),
        ],
        out_specs=pl.BlockSpec((tm, d), lambda i: (i, 0)),
        out_shape=jax.ShapeDtypeStruct((m, d), F32),
        scratch_shapes=[pltpu.VMEM((tm, SGU_W), BF16)],
        compiler_params=pltpu.CompilerParams(
            dimension_semantics=("parallel",),
            vmem_limit_bytes=V7X_VMEM_LIMIT_BYTES),
        name="outproj",
    )(y_ssm, z_rest, z_rest, z_rest, sgu_ln_g, sgu_ln_b, sgu_w, sgu_bt, y_att, w_out, x2d, final_g)


def _pick(n, prefs):
    for p in prefs:
        if n % p == 0:
            return p
    raise ValueError(f"no supported tile for extent {n}")


def kernel(x, norm_g, w_in, ssm_a_re, ssm_a_im, ssm_log_step, ssm_b_re, ssm_b_im, ssm_c_re, ssm_c_im,
           ssm_d, glu_w, glu_b, sgu_ln_g, sgu_ln_b, sgu_w, sgu_b, lam_q1, lam_k1, lam_q2, lam_k2,
           attn_subln_g, w_out, final_g):
    nbatch, seq, d = x.shape
    depth = norm_g.shape[0]
    assert d == D_MODEL and seq % 512 == 0 and nbatch == 8
    m = nbatch * seq
    tm_in = _pick(m, (512, 256))
    tt = _pick(seq, (64, 32))
    tq = 512
    q_lo = SSM_COLS + 3 * SGU_W
    col_scale = jnp.ones((SSM_COLS + REST_COLS,), F32).at[q_lo:q_lo + ATT_W].set(
        ATT_HEAD_D ** -0.5 * LOG2_E)

    x2d = x.reshape(m, d).astype(F32)
    fg = final_g.reshape(1, d).astype(F32)
    for l in range(depth):
        g = norm_g[l].reshape(1, d).astype(F32)
        w_l = (w_in[l].astype(F32) * col_scale).astype(BF16)
        lam_init = 0.8 - 0.6 * math.exp(-0.3 * l)

        z_rest = _inproj(x2d, g, w_l[:, SSM_COLS:], tm=tm_in)

        ssm_p = _ssm_params(ssm_a_re[l], ssm_a_im[l], ssm_log_step[l], ssm_b_re[l], ssm_b_im[l],
                            ssm_c_re[l], ssm_c_im[l], nbatch)
        x_tb = x2d.reshape(nbatch, seq, d).swapaxes(0, 1).reshape(m, d)
        y_ssm_tb = _ssm_branch(x_tb, g, w_l[:, :SSM_COLS], *ssm_p,
                               ssm_d[l].reshape(1, SSM_W).astype(F32), glu_w[l].astype(BF16),
                               glu_b[l].reshape(1, SSM_W).astype(F32), tt=tt, nb=nbatch)
        y_ssm = y_ssm_tb.reshape(seq, nbatch, SSM_W).swapaxes(0, 1).reshape(m, SSM_W)

        y_att = _attention(z_rest, lam_q1[l].reshape(1, -1).astype(F32), lam_k1[l].reshape(1, -1).astype(F32),
                           lam_q2[l].reshape(1, -1).astype(F32), lam_k2[l].reshape(1, -1).astype(F32),
                           attn_subln_g[l].reshape(1, ATT_HEAD_W).astype(F32),
                           nbatch=nbatch, seq=seq, tq=tq, lam_init=lam_init)

        x2d = _outproj(y_ssm, z_rest, sgu_ln_g[l].reshape(1, SGU_W).astype(F32),
                       sgu_ln_b[l].reshape(1, SGU_W).astype(F32), sgu_w[l].astype(F32),
                       sgu_b[l].astype(F32).T, y_att, w_out[l].astype(BF16), x2d, fg,
                       tm=512, final_norm=(l == depth - 1))
    return x2d.reshape(nbatch, seq, d).astype(x.dtype)
```

```python
import functools
import math

import jax
import jax.numpy as jnp
from jax import lax
from jax.experimental import pallas as pl
from jax.experimental.pallas import tpu as pltpu

F32 = jnp.float32
BF16 = jnp.bfloat16

D_MODEL = 1024
D_MIX = 2 * D_MODEL
SSM_W = D_MIX // 4
SSM_GROUP_CH = 16
SSM_GROUPS = SSM_W // SSM_GROUP_CH
SSM_STATE = 64
SSM_CH = SSM_GROUPS * SSM_STATE
SSM_HALF_IN = SSM_W // 2
SSM_HALF_CH = SSM_CH // 2
SGU_W = D_MIX // 4
CHUNK = 128
SGU_HEADS = 4
SGU_HEAD_W = SGU_W // SGU_HEADS
ATT_W = D_MIX // 2
ATT_HEAD_D = 64
ATT_HEADS = ATT_W // (2 * ATT_HEAD_D)
ATT_HEAD_W = 2 * ATT_HEAD_D
ATT_PAD_ROWS = 16
NORM_EPS = 1e-6
LN_EPS = 1e-5

SSM_COLS = 2 * SSM_W
REST_COLS = 3 * SGU_W + 4 * ATT_W
SGU_U_BLK, SGU_V_BLK, SGU_G_BLK = 0, 1, 2
ATT_Q_OFF = 3 * SGU_W // ATT_HEAD_W
ATT_K_OFF = ATT_Q_OFF + ATT_HEADS
ATT_V_OFF = ATT_K_OFF + ATT_HEADS
ATT_G_OFF = ATT_V_OFF + ATT_HEADS

V7X_VMEM_LIMIT_BYTES = 56 * 1024 * 1024
MXU_COL_CHUNK = 1024
LOG2_E = math.log2(math.e)
MASK_VALUE = -1e30


def _rms(x, g):
    ms = jnp.mean(x * x, axis=-1, keepdims=True)
    return x * lax.rsqrt(ms + NORM_EPS) * g


def _silu(x):
    return x * (1.0 / (1.0 + jnp.exp(-x)))


def _sigmoid(x):
    return 1.0 / (1.0 + jnp.exp(-x))


def _gelu_exact(x):
    return 0.5 * x * (1.0 + lax.erf(x * math.sqrt(0.5)))


def _inproj_kernel(x_ref, g_ref, w_ref, o_ref, *, col_chunks):
    h = _rms(x_ref[...], g_ref[...]).astype(BF16)
    c0 = 0
    for cw in col_chunks:
        o_ref[:, c0:c0 + cw] = jnp.dot(h, w_ref[:, c0:c0 + cw],
                                       preferred_element_type=F32).astype(o_ref.dtype)
        c0 += cw


def _resident(*shape):
    return pl.BlockSpec(shape, lambda *_: (0,) * len(shape), pipeline_mode=pl.Buffered(1))


def _inproj(x2d, g, w, *, tm):
    m, d = x2d.shape
    n = w.shape[1]
    col_chunks = (MXU_COL_CHUNK,) * (n // MXU_COL_CHUNK)
    if n % MXU_COL_CHUNK:
        col_chunks += (n % MXU_COL_CHUNK,)
    return pl.pallas_call(
        functools.partial(_inproj_kernel, col_chunks=col_chunks),
        grid=(m // tm,),
        in_specs=[
            pl.BlockSpec((tm, d), lambda i: (i, 0)),
            _resident(1, d),
            _resident(d, n),
        ],
        out_specs=pl.BlockSpec((tm, n), lambda i: (i, 0)),
        out_shape=jax.ShapeDtypeStruct((m, n), BF16),
        compiler_params=pltpu.CompilerParams(
            dimension_semantics=("parallel",),
            vmem_limit_bytes=V7X_VMEM_LIMIT_BYTES),
        name="inproj",
    )(x2d, g, w)


def _ssm_kernel(x_ref, g_ref, w_ref, are_ref, aim_ref, bbd_ref, cre_ref, ncim_ref, d_ref,
                gw_ref, gb_ref, o_ref, bre_ref, bim_ref, carry_ref, *, tt, nb):
    @pl.when(pl.program_id(0) == 0)
    def _():
        carry_ref[...] = jnp.zeros_like(carry_ref)

    h = _rms(x_ref[...], g_ref[...]).astype(BF16)
    z = jnp.dot(h, w_ref[...], preferred_element_type=F32)
    u = z[:, :SSM_W]
    gate = z[:, SSM_W:]
    ub = u.astype(BF16)

    for hf in range(2):
        r = jnp.dot(ub[:, hf * SSM_HALF_IN:(hf + 1) * SSM_HALF_IN], bbd_ref[hf],
                    preferred_element_type=F32)
        bre_ref[:, hf * SSM_HALF_CH:(hf + 1) * SSM_HALF_CH] = r[:, :SSM_HALF_CH]
        bim_ref[:, hf * SSM_HALF_CH:(hf + 1) * SSM_HALF_CH] = r[:, SSM_HALF_CH:]

    def step(t, carry):
        sr, si = carry
        rows = pl.ds(pl.multiple_of(t * nb, nb), nb)
        ar = are_ref[...]
        ai = aim_ref[...]
        nsr = ar * sr - ai * si + bre_ref[rows, :]
        nsi = ar * si + ai * sr + bim_ref[rows, :]
        bre_ref[rows, :] = nsr
        bim_ref[rows, :] = nsi
        return nsr, nsi

    sr, si = lax.fori_loop(0, tt, step, (carry_ref[0], carry_ref[1]))
    carry_ref[0] = sr
    carry_ref[1] = si

    ys = []
    for hf in range(2):
        cols = slice(hf * SSM_HALF_CH, (hf + 1) * SSM_HALF_CH)
        yh = jnp.dot(bre_ref[:, cols].astype(BF16), cre_ref[hf], preferred_element_type=F32)
        yh = yh + jnp.dot(bim_ref[:, cols].astype(BF16), ncim_ref[hf], preferred_element_type=F32)
        ys.append(yh)
    y = jnp.concatenate(ys, axis=-1) + d_ref[...] * u
    y = _gelu_exact(y)
    y = y * _sigmoid(jnp.dot(y.astype(BF16), gw_ref[...], preferred_element_type=F32) + gb_ref[...])
    o_ref[...] = (y * _silu(gate)).astype(o_ref.dtype)


def _ssm_branch(x_tb, g, w_ssm, are8, aim8, bbd, cre_bd, ncim_bd, d, glu_w, glu_b, *, tt, nb):
    m, dm = x_tb.shape
    r = tt * nb
    const = lambda *shape: pl.BlockSpec(shape, lambda i: (0,) * len(shape))
    return pl.pallas_call(
        functools.partial(_ssm_kernel, tt=tt, nb=nb),
        grid=(m // r,),
        in_specs=[
            pl.BlockSpec((r, dm), lambda i: (i, 0)),
            const(1, dm),
            const(dm, SSM_COLS),
            const(nb, SSM_CH),
            const(nb, SSM_CH),
            const(2, SSM_HALF_IN, 2 * SSM_HALF_CH),
            const(2, SSM_HALF_CH, SSM_HALF_IN),
            const(2, SSM_HALF_CH, SSM_HALF_IN),
            const(1, SSM_W),
            const(SSM_W, SSM_W),
            const(1, SSM_W),
        ],
        out_specs=pl.BlockSpec((r, SSM_W), lambda i: (i, 0)),
        out_shape=jax.ShapeDtypeStruct((m, SSM_W), BF16),
        scratch_shapes=[
            pltpu.VMEM((r, SSM_CH), F32),
            pltpu.VMEM((r, SSM_CH), F32),
            pltpu.VMEM((2, nb, SSM_CH), F32),
        ],
        compiler_params=pltpu.CompilerParams(
            dimension_semantics=("arbitrary",),
            vmem_limit_bytes=V7X_VMEM_LIMIT_BYTES),
        name="ssm_branch",
    )(x_tb, g, w_ssm, are8, aim8, bbd, cre_bd, ncim_bd, d, glu_w, glu_b)


def _ssm_params(a_re, a_im, log_step, b_re, b_im, c_re, c_im, nb):
    lr = a_re.astype(F32)
    li = a_im.astype(F32)
    step = jnp.exp(log_step.astype(F32))[:, None]
    mag = jnp.exp(step * lr)
    ang = step * li
    ab_re = mag * jnp.cos(ang)
    ab_im = mag * jnp.sin(ang)
    den = lr * lr + li * li
    nr = ab_re - 1.0
    ni = ab_im
    co_re = (nr * lr + ni * li) / den
    co_im = (ni * lr - nr * li) / den
    br = b_re.astype(F32)
    bi = b_im.astype(F32)
    bb_re = co_re[..., None] * br - co_im[..., None] * bi
    bb_im = co_re[..., None] * bi + co_im[..., None] * br
    gh = SSM_GROUPS // 2
    eye = jnp.eye(gh, dtype=F32)

    def b_blockdiag(bb):
        t = bb.reshape(2, gh, SSM_STATE, SSM_GROUP_CH)
        t = jnp.einsum('fgph,gk->fghkp', t, eye)
        return t.reshape(2, gh * SSM_GROUP_CH, gh * SSM_STATE)

    def c_blockdiag(c):
        t = c.astype(F32).reshape(2, gh, SSM_GROUP_CH, SSM_STATE)
        t = jnp.einsum('fghp,gk->fgpkh', t, eye)
        return t.reshape(2, gh * SSM_STATE, gh * SSM_GROUP_CH)

    bbd = jnp.concatenate([b_blockdiag(bb_re), b_blockdiag(bb_im)], axis=-1).astype(BF16)
    cre_bd = c_blockdiag(c_re).astype(BF16)
    ncim_bd = (-c_blockdiag(c_im)).astype(BF16)
    are8 = jnp.broadcast_to(ab_re.reshape(1, SSM_CH), (nb, SSM_CH))
    aim8 = jnp.broadcast_to(ab_im.reshape(1, SSM_CH), (nb, SSM_CH))
    return are8, aim8, bbd, cre_bd, ncim_bd


def _sgu_mix(u_ref, v_ref, gt_ref, lg_ref, lb_ref, w_ref, bt_ref, o_ref, tm):
    v = v_ref[...].astype(F32)
    mu = jnp.mean(v, axis=-1, keepdims=True)
    vc = v - mu
    var = jnp.mean(vc * vc, axis=-1, keepdims=True)
    vn = (vc * lax.rsqrt(var + LN_EPS) * lg_ref[...] + lb_ref[...]).astype(BF16)
    row = lax.broadcasted_iota(jnp.int32, (CHUNK, CHUNK), 0)
    col = lax.broadcasted_iota(jnp.int32, (CHUNK, CHUNK), 1)
    for hd in range(SGU_HEADS):
        w = jnp.where(col <= row, w_ref[hd], 0.0).astype(BF16)
        bias = bt_ref[:, hd:hd + 1]
        cs = slice(hd * SGU_HEAD_W, (hd + 1) * SGU_HEAD_W)
        for c in range(tm // CHUNK):
            rs = slice(c * CHUNK, (c + 1) * CHUNK)
            s = jnp.dot(w, vn[rs, cs], preferred_element_type=F32) + bias
            uu = u_ref[rs, cs].astype(F32)
            gg = gt_ref[rs, cs].astype(F32)
            o_ref[rs, cs] = (uu * s * _silu(gg)).astype(o_ref.dtype)


def _attn_kernel(q_ref, k_ref, v_ref, gt_ref, lq1_ref, lk1_ref, lq2_ref, lk2_ref, sg_ref,
                 o_ref, qzt_ref, vt_ref, s_ref, m_ref, acc_ref, fin_ref, *, tq, lam_init):
    nq = vt_ref.shape[0]

    pad = lax.broadcasted_iota(jnp.int32, (ATT_PAD_ROWS, tq), 0)
    ones_row = jnp.where(pad == 0, 1.0, 0.0).astype(BF16)
    for j in range(nq):
        vt_ref[j, :ATT_HEAD_W, :] = v_ref[j * tq:(j + 1) * tq, :].T
        vt_ref[j, ATT_HEAD_W:, :] = ones_row

    lam = (jnp.exp(jnp.sum(lq1_ref[...] * lk1_ref[...], axis=-1, keepdims=True))
           - jnp.exp(jnp.sum(lq2_ref[...] * lk2_ref[...], axis=-1, keepdims=True)) + lam_init)

    def block_rows(i):
        return pl.ds(pl.multiple_of(i * tq, tq), tq)

    def prep_q(qi):
        qt = q_ref[block_rows(qi), :].T
        sub = lax.broadcasted_iota(jnp.int32, (ATT_HEAD_W, tq), 0)
        qzt_ref[0] = jnp.where(sub < ATT_HEAD_D, qt, 0).astype(BF16)
        qzt_ref[1] = jnp.where(sub >= ATT_HEAD_D, qt, 0).astype(BF16)

    def scores(st, kb):
        s_ref[st] = jnp.dot(k_ref[block_rows(kb), :], qzt_ref[st],
                            preferred_element_type=F32)

    def softmax_pv(st, kb, masked):
        s = s_ref[st]
        if masked:
            key = lax.broadcasted_iota(jnp.int32, (tq, tq), 0)
            qry = lax.broadcasted_iota(jnp.int32, (tq, tq), 1)
            s = jnp.where(key <= qry, s, MASK_VALUE)
        m = m_ref[st]
        m_new = jnp.maximum(m, jnp.max(s, axis=0, keepdims=True))
        alpha = jnp.exp2(m - m_new)
        p = jnp.exp2(s - m_new).astype(BF16)
        acc_ref[st] = alpha * acc_ref[st] + jnp.dot(vt_ref[kb], p, preferred_element_type=F32)
        m_ref[st] = m_new

    def reset_stats():
        m_ref[...] = jnp.full(m_ref.shape, MASK_VALUE, F32)
        acc_ref[...] = jnp.zeros(acc_ref.shape, F32)

    def finish(qi):
        a1 = fin_ref[0]
        a2 = fin_ref[1]
        o1 = a1[:ATT_HEAD_W] * (1.0 / a1[ATT_HEAD_W:ATT_HEAD_W + 1])
        o2 = a2[:ATT_HEAD_W] * (1.0 / a2[ATT_HEAD_W:ATT_HEAD_W + 1])
        ot = o1 - lam * o2
        o = _rms(ot.T, sg_ref[...]) * (1.0 - lam_init)
        rows = block_rows(qi)
        o_ref[rows, :] = (o * _silu(gt_ref[rows, :].astype(F32))).astype(o_ref.dtype)

    prep_q(0)
    reset_stats()
    den = lax.broadcasted_iota(jnp.int32, fin_ref.shape, 1)
    fin_ref[...] = jnp.where(den == ATT_HEAD_W, 1.0, 0.0)
    scores(0, 0)

    def q_block(qi, c):
        def k_block(kb):
            scores(1, kb)
            softmax_pv(0, kb, False)
            scores(0, kb + 1)
            softmax_pv(1, kb, False)

        def k_pair(j, c2):
            k_block(2 * j)
            k_block(2 * j + 1)
            return c2

        def k_single(j, c2):
            k_block(qi - 1)
            return c2

        lax.fori_loop(0, qi // 2, k_pair, 0)
        lax.fori_loop(0, qi % 2, k_single, 0)
        scores(1, qi)
        softmax_pv(0, qi, True)
        prep_q(jnp.minimum(qi + 1, nq - 1))
        scores(0, 0)
        finish(jnp.maximum(qi - 1, 0))
        softmax_pv(1, qi, True)
        fin_ref[...] = acc_ref[...]
        reset_stats()
        return c

    lax.fori_loop(0, nq, q_block, 0)
    finish(nq - 1)


def _attention(z_rest, lq1, lk1, lq2, lk2, subln_g, *, nbatch, seq, tq, lam_init):
    nq = seq // tq
    const = lambda *shape: pl.BlockSpec(shape, lambda b, h: (0,) * len(shape))
    return pl.pallas_call(
        functools.partial(_attn_kernel, tq=tq, lam_init=lam_init),
        grid=(nbatch, ATT_HEADS),
        in_specs=[
            pl.BlockSpec((seq, ATT_HEAD_W), lambda b, h: (b, ATT_Q_OFF + h)),
            pl.BlockSpec((seq, ATT_HEAD_W), lambda b, h: (b, ATT_K_OFF + h)),
            pl.BlockSpec((seq, ATT_HEAD_W), lambda b, h: (b, ATT_V_OFF + h)),
            pl.BlockSpec((seq, ATT_HEAD_W), lambda b, h: (b, ATT_G_OFF + h)),
            const(1, ATT_HEAD_D),
            const(1, ATT_HEAD_D),
            const(1, ATT_HEAD_D),
            const(1, ATT_HEAD_D),
            const(1, ATT_HEAD_W),
        ],
        out_specs=pl.BlockSpec((seq, ATT_HEAD_W), lambda b, h: (b, h)),
        out_shape=jax.ShapeDtypeStruct((nbatch * seq, ATT_W), BF16),
        scratch_shapes=[pltpu.VMEM((2, ATT_HEAD_W, tq), BF16),
                        pltpu.VMEM((nq, ATT_HEAD_W + ATT_PAD_ROWS, tq), BF16),
                        pltpu.VMEM((2, tq, tq), F32),
                        pltpu.VMEM((2, 1, tq), F32),
                        pltpu.VMEM((2, ATT_HEAD_W + ATT_PAD_ROWS, tq), F32),
                        pltpu.VMEM((2, ATT_HEAD_W + ATT_PAD_ROWS, tq), F32)],
        compiler_params=pltpu.CompilerParams(
            dimension_semantics=("parallel", "parallel"),
            vmem_limit_bytes=V7X_VMEM_LIMIT_BYTES),
        name="diff_attention",
    )(z_rest, z_rest, z_rest, z_rest, lq1, lk1, lq2, lk2, subln_g)


def _outproj_kernel(ys_ref, u_ref, v_ref, gt_ref, lg_ref, lb_ref, ws_ref, bt_ref, ya_ref, w_ref,
                    x_ref, fg_ref, o_ref, yg_ref, *, tm, final_norm):
    _sgu_mix(u_ref, v_ref, gt_ref, lg_ref, lb_ref, ws_ref, bt_ref, yg_ref, tm)
    acc = jnp.dot(ys_ref[...], w_ref[:SSM_W, :], preferred_element_type=F32)
    acc = acc + jnp.dot(yg_ref[...], w_ref[SSM_W:SSM_W + SGU_W, :], preferred_element_type=F32)
    acc = acc + jnp.dot(ya_ref[...], w_ref[SSM_W + SGU_W:, :], preferred_element_type=F32)
    xn = x_ref[...] + acc
    if final_norm:
        xn = _rms(xn, fg_ref[...])
    o_ref[...] = xn


def _outproj(y_ssm, z_rest, sgu_ln_g, sgu_ln_b, sgu_w, sgu_bt, y_att, w_out, x2d, final_g, *,
             tm, final_norm):
    m, d = x2d.shape
    assert tm % CHUNK == 0
    return pl.pallas_call(
        functools.partial(_outproj_kernel, tm=tm, final_norm=final_norm),
        grid=(m // tm,),
        in_specs=[
            pl.BlockSpec((tm, SSM_W), lambda i: (i, 0)),
            pl.BlockSpec((tm, SGU_W), lambda i: (i, SGU_U_BLK)),
            pl.BlockSpec((tm, SGU_W), lambda i: (i, SGU_V_BLK)),
            pl.BlockSpec((tm, SGU_W), lambda i: (i, SGU_G_BLK)),
            _resident(1, SGU_W),
            _resident(1, SGU_W),
            _resident(SGU_HEADS, CHUNK, CHUNK),
            _resident(CHUNK, SGU_HEADS),
            pl.BlockSpec((tm, ATT_W), lambda i: (i, 0)),
            _resident(D_MIX, d),
            pl.BlockSpec((tm, d), lambda i: (i, 0)),
            _resident(1, d),
        ],
        out_specs=pl.BlockSpec((tm, d), lambda i: (i, 0)),
        out_shape=jax.ShapeDtypeStruct((m, d), F32),
        scratch_shapes=[pltpu.VMEM((tm, SGU_W), BF16)],
        compiler_params=pltpu.CompilerParams(
            dimension_semantics=("parallel",),
            vmem_limit_bytes=V7X_VMEM_LIMIT_BYTES),
        name="outproj",
    )(y_ssm, z_rest, z_rest, z_rest, sgu_ln_g, sgu_ln_b, sgu_w, sgu_bt, y_att, w_out, x2d, final_g)


def _pick(n, prefs):
    for p in prefs:
        if n % p == 0:
            return p
    raise ValueError(f"no supported tile for extent {n}")


def kernel(x, norm_g, w_in, ssm_a_re, ssm_a_im, ssm_log_step, ssm_b_re, ssm_b_im, ssm_c_re, ssm_c_im,
           ssm_d, glu_w, glu_b, sgu_ln_g, sgu_ln_b, sgu_w, sgu_b, lam_q1, lam_k1, lam_q2, lam_k2,
           attn_subln_g, w_out, final_g):
    nbatch, seq, d = x.shape
    depth = norm_g.shape[0]
    assert d == D_MODEL and seq % 512 == 0 and nbatch == 8
    m = nbatch * seq
    tm_in = _pick(m, (512, 256))
    tt = _pick(seq, (64, 32))
    tq = 512
    q_lo = SSM_COLS + 3 * SGU_W
    col_scale = jnp.ones((SSM_COLS + REST_COLS,), F32).at[q_lo:q_lo + ATT_W].set(
        ATT_HEAD_D ** -0.5 * LOG2_E)

    x2d = x.reshape(m, d).astype(F32)
    fg = final_g.reshape(1, d).astype(F32)
    for l in range(depth):
        g = norm_g[l].reshape(1, d).astype(F32)
        w_l = (w_in[l].astype(F32) * col_scale).astype(BF16)
        lam_init = 0.8 - 0.6 * math.exp(-0.3 * l)

        z_rest = _inproj(x2d, g, w_l[:, SSM_COLS:], tm=tm_in)

        ssm_p = _ssm_params(ssm_a_re[l], ssm_a_im[l], ssm_log_step[l], ssm_b_re[l], ssm_b_im[l],
                            ssm_c_re[l], ssm_c_im[l], nbatch)
        x_tb = x2d.reshape(nbatch, seq, d).swapaxes(0, 1).reshape(m, d)
        y_ssm_tb = _ssm_branch(x_tb, g, w_l[:, :SSM_COLS], *ssm_p,
                               ssm_d[l].reshape(1, SSM_W).astype(F32), glu_w[l].astype(BF16),
                               glu_b[l].reshape(1, SSM_W).astype(F32), tt=tt, nb=nbatch)
        y_ssm = y_ssm_tb.reshape(seq, nbatch, SSM_W).swapaxes(0, 1).reshape(m, SSM_W)

        y_att = _attention(z_rest, lam_q1[l].reshape(1, -1).astype(F32), lam_k1[l].reshape(1, -1).astype(F32),
                           lam_q2[l].reshape(1, -1).astype(F32), lam_k2[l].reshape(1, -1).astype(F32),
                           attn_subln_g[l].reshape(1, ATT_HEAD_W).astype(F32),
                           nbatch=nbatch, seq=seq, tq=tq, lam_init=lam_init)

        x2d = _outproj(y_ssm, z_rest, sgu_ln_g[l].reshape(1, SGU_W).astype(F32),
                       sgu_ln_b[l].reshape(1, SGU_W).astype(F32), sgu_w[l].astype(F32),
                       sgu_b[l].astype(F32).T, y_att, w_out[l].astype(BF16), x2d, fg,
                       tm=512, final_norm=(l == depth - 1))
    return x2d.reshape(nbatch, seq, d).astype(x.dtype)
```

```python
import functools
import math

import jax
import jax.numpy as jnp
from jax import lax
from jax.experimental import pallas as pl
from jax.experimental.pallas import tpu as pltpu

F32 = jnp.float32
BF16 = jnp.bfloat16

D_MODEL = 1024
D_MIX = 2 * D_MODEL
SSM_W = D_MIX // 4
SSM_GROUP_CH = 16
SSM_GROUPS = SSM_W // SSM_GROUP_CH
SSM_STATE = 64
SSM_CH = SSM_GROUPS * SSM_STATE
SSM_HALF_IN = SSM_W // 2
SSM_HALF_CH = SSM_CH // 2
SGU_W = D_MIX // 4
CHUNK = 128
SGU_HEADS = 4
SGU_HEAD_W = SGU_W // SGU_HEADS
ATT_W = D_MIX // 2
ATT_HEAD_D = 64
ATT_HEADS = ATT_W // (2 * ATT_HEAD_D)
ATT_HEAD_W = 2 * ATT_HEAD_D
ATT_PAD_ROWS = 16
NORM_EPS = 1e-6
LN_EPS = 1e-5

SSM_COLS = 2 * SSM_W
REST_COLS = 3 * SGU_W + 4 * ATT_W
SGU_U_BLK, SGU_V_BLK, SGU_G_BLK = 0, 1, 2
ATT_Q_OFF = 3 * SGU_W // ATT_HEAD_W
ATT_K_OFF = ATT_Q_OFF + ATT_HEADS
ATT_V_OFF = ATT_K_OFF + ATT_HEADS
ATT_G_OFF = ATT_V_OFF + ATT_HEADS

V7X_VMEM_LIMIT_BYTES = 56 * 1024 * 1024
MXU_COL_CHUNK = 1024
LOG2_E = math.log2(math.e)
MASK_VALUE = -1e30


def _rms(x, g):
    ms = jnp.mean(x * x, axis=-1, keepdims=True)
    return x * lax.rsqrt(ms + NORM_EPS) * g


def _silu(x):
    return x * (1.0 / (1.0 + jnp.exp(-x)))


def _sigmoid(x):
    return 1.0 / (1.0 + jnp.exp(-x))


def _gelu_exact(x):
    return 0.5 * x * (1.0 + lax.erf(x * math.sqrt(0.5)))


def _inproj_kernel(x_ref, g_ref, w_ref, o_ref, *, col_chunks):
    h = _rms(x_ref[...], g_ref[...]).astype(BF16)
    c0 = 0
    for cw in col_chunks:
        o_ref[:, c0:c0 + cw] = jnp.dot(h, w_ref[:, c0:c0 + cw],
                                       preferred_element_type=F32).astype(o_ref.dtype)
        c0 += cw


def _resident(*shape):
    return pl.BlockSpec(shape, lambda *_: (0,) * len(shape), pipeline_mode=pl.Buffered(1))


def _inproj(x2d, g, w, *, tm):
    m, d = x2d.shape
    n = w.shape[1]
    col_chunks = (MXU_COL_CHUNK,) * (n // MXU_COL_CHUNK)
    if n % MXU_COL_CHUNK:
        col_chunks += (n % MXU_COL_CHUNK,)
    return pl.pallas_call(
        functools.partial(_inproj_kernel, col_chunks=col_chunks),
        grid=(m // tm,),
        in_specs=[
            pl.BlockSpec((tm, d), lambda i: (i, 0)),
            _resident(1, d),
            _resident(d, n),
        ],
        out_specs=pl.BlockSpec((tm, n), lambda i: (i, 0)),
        out_shape=jax.ShapeDtypeStruct((m, n), BF16),
        compiler_params=pltpu.CompilerParams(
            dimension_semantics=("parallel",),
            vmem_limit_bytes=V7X_VMEM_LIMIT_BYTES),
        name="inproj",
    )(x2d, g, w)


def _ssm_kernel(x_hbm, g_ref, w_ref, are_ref, aim_ref, bbd_ref, cre_ref, ncim_ref, d_ref,
                gw_ref, gb_ref, o_ref, xbuf, sem, z_ref, bre_ref, bim_ref, carry_ref,
                *, tt, nb, nblk):
    j = pl.program_id(0)
    r = tt * nb

    def x_copies(slot, blk):
        return [pltpu.make_async_copy(x_hbm.at[b, pl.ds(blk * tt, tt), :], xbuf.at[slot, :, b, :],
                                      sem.at[slot, b]) for b in range(nb)]

    def fetch(slot, blk):
        for c in x_copies(slot, blk):
            c.start()

    def await_x(slot, blk):
        for c in x_copies(slot, blk):
            c.wait()

    def project(slot):
        x = xbuf[slot].reshape(r, x_hbm.shape[-1])
        h = _rms(x, g_ref[...]).astype(BF16)
        z = jnp.dot(h, w_ref[...], preferred_element_type=F32)
        z_ref[slot] = z
        ub = z[:, :SSM_W].astype(BF16)
        for hf in range(2):
            bu = jnp.dot(ub[:, hf * SSM_HALF_IN:(hf + 1) * SSM_HALF_IN], bbd_ref[hf],
                         preferred_element_type=F32)
            bre_ref[slot, :, hf * SSM_HALF_CH:(hf + 1) * SSM_HALF_CH] = bu[:, :SSM_HALF_CH]
            bim_ref[slot, :, hf * SSM_HALF_CH:(hf + 1) * SSM_HALF_CH] = bu[:, SSM_HALF_CH:]

    def scan(slot):
        sr = carry_ref[0]
        si = carry_ref[1]
        ar = are_ref[...]
        ai = aim_ref[...]
        for t in range(tt):
            rows = slice(t * nb, (t + 1) * nb)
            nsr = ar * sr - ai * si + bre_ref[slot, rows, :]
            nsi = ar * si + ai * sr + bim_ref[slot, rows, :]
            bre_ref[slot, rows, :] = nsr
            bim_ref[slot, rows, :] = nsi
            sr, si = nsr, nsi
        carry_ref[0] = sr
        carry_ref[1] = si

    def emit(slot):
        u = z_ref[slot, :, :SSM_W]
        gate = z_ref[slot, :, SSM_W:]
        ys = []
        for hf in range(2):
            cols = slice(hf * SSM_HALF_CH, (hf + 1) * SSM_HALF_CH)
            yh = jnp.dot(bre_ref[slot, :, cols].astype(BF16), cre_ref[hf], preferred_element_type=F32)
            yh = yh + jnp.dot(bim_ref[slot, :, cols].astype(BF16), ncim_ref[hf],
                              preferred_element_type=F32)
            ys.append(yh)
        y = jnp.concatenate(ys, axis=-1) + d_ref[...] * u
        y = _gelu_exact(y)
        y = y * _sigmoid(jnp.dot(y.astype(BF16), gw_ref[...], preferred_element_type=F32) + gb_ref[...])
        o_ref[slot * r:(slot + 1) * r, :] = (y * _silu(gate)).astype(o_ref.dtype)

    @pl.when(j == 0)
    def _():
        carry_ref[...] = jnp.zeros_like(carry_ref)
        fetch(0, 0)
        await_x(0, 0)
        project(0)
        fetch(1, 1)

    has_next = 2 * j + 2 < nblk

    @pl.when(has_next)
    def _():
        fetch(0, 2 * j + 2)

    await_x(1, 2 * j + 1)
    project(1)
    scan(0)
    emit(0)

    @pl.when(has_next)
    def _():
        fetch(1, 2 * j + 3)
        await_x(0, 2 * j + 2)

    project(0)
    scan(1)
    emit(1)


def _ssm_branch(x3d, g, w_ssm, are8, aim8, bbd, cre_bd, ncim_bd, d, glu_w, glu_b, *, tt):
    nb, seq, dm = x3d.shape
    r = tt * nb
    nblk = seq // tt
    assert nblk % 2 == 0
    return pl.pallas_call(
        functools.partial(_ssm_kernel, tt=tt, nb=nb, nblk=nblk),
        grid=(nblk // 2,),
        in_specs=[
            pl.BlockSpec(memory_space=pl.ANY),
            _resident(1, dm),
            _resident(dm, SSM_COLS),
            _resident(nb, SSM_CH),
            _resident(nb, SSM_CH),
            _resident(2, SSM_HALF_IN, 2 * SSM_HALF_CH),
            _resident(2, SSM_HALF_CH, SSM_HALF_IN),
            _resident(2, SSM_HALF_CH, SSM_HALF_IN),
            _resident(1, SSM_W),
            _resident(SSM_W, SSM_W),
            _resident(1, SSM_W),
        ],
        out_specs=pl.BlockSpec((2 * r, SSM_W), lambda i: (i, 0)),
        out_shape=jax.ShapeDtypeStruct((seq * nb, SSM_W), BF16),
        scratch_shapes=[
            pltpu.VMEM((2, tt, nb, dm), F32),
            pltpu.SemaphoreType.DMA((2, nb)),
            pltpu.VMEM((2, r, SSM_COLS), F32),
            pltpu.VMEM((2, r, SSM_CH), F32),
            pltpu.VMEM((2, r, SSM_CH), F32),
            pltpu.VMEM((2, nb, SSM_CH), F32),
        ],
        compiler_params=pltpu.CompilerParams(
            dimension_semantics=("arbitrary",),
            vmem_limit_bytes=V7X_VMEM_LIMIT_BYTES),
        name="ssm_branch",
    )(x3d, g, w_ssm, are8, aim8, bbd, cre_bd, ncim_bd, d, glu_w, glu_b)


def _ssm_params(a_re, a_im, log_step, b_re, b_im, c_re, c_im, nb):
    lr = a_re.astype(F32)
    li = a_im.astype(F32)
    step = jnp.exp(log_step.astype(F32))[:, None]
    mag = jnp.exp(step * lr)
    ang = step * li
    ab_re = mag * jnp.cos(ang)
    ab_im = mag * jnp.sin(ang)
    den = lr * lr + li * li
    nr = ab_re - 1.0
    ni = ab_im
    co_re = (nr * lr + ni * li) / den
    co_im = (ni * lr - nr * li) / den
    br = b_re.astype(F32)
    bi = b_im.astype(F32)
    bb_re = co_re[..., None] * br - co_im[..., None] * bi
    bb_im = co_re[..., None] * bi + co_im[..., None] * br
    gh = SSM_GROUPS // 2
    eye = jnp.eye(gh, dtype=F32)

    def b_blockdiag(bb):
        t = bb.reshape(2, gh, SSM_STATE, SSM_GROUP_CH)
        t = jnp.einsum('fgph,gk->fghkp', t, eye)
        return t.reshape(2, gh * SSM_GROUP_CH, gh * SSM_STATE)

    def c_blockdiag(c):
        t = c.astype(F32).reshape(2, gh, SSM_GROUP_CH, SSM_STATE)
        t = jnp.einsum('fghp,gk->fgpkh', t, eye)
        return t.reshape(2, gh * SSM_STATE, gh * SSM_GROUP_CH)

    bbd = jnp.concatenate([b_blockdiag(bb_re), b_blockdiag(bb_im)], axis=-1).astype(BF16)
    cre_bd = c_blockdiag(c_re).astype(BF16)
    ncim_bd = (-c_blockdiag(c_im)).astype(BF16)
    are8 = jnp.broadcast_to(ab_re.reshape(1, SSM_CH), (nb, SSM_CH))
    aim8 = jnp.broadcast_to(ab_im.reshape(1, SSM_CH), (nb, SSM_CH))
    return are8, aim8, bbd, cre_bd, ncim_bd


def _sgu_mix(u_ref, v_ref, gt_ref, lg_ref, lb_ref, w_ref, bt_ref, o_ref, tm):
    v = v_ref[...].astype(F32)
    mu = jnp.mean(v, axis=-1, keepdims=True)
    vc = v - mu
    var = jnp.mean(vc * vc, axis=-1, keepdims=True)
    vn = (vc * lax.rsqrt(var + LN_EPS) * lg_ref[...] + lb_ref[...]).astype(BF16)
    row = lax.broadcasted_iota(jnp.int32, (CHUNK, CHUNK), 0)
    col = lax.broadcasted_iota(jnp.int32, (CHUNK, CHUNK), 1)
    for hd in range(SGU_HEADS):
        w = jnp.where(col <= row, w_ref[hd], 0.0).astype(BF16)
        bias = bt_ref[:, hd:hd + 1]
        cs = slice(hd * SGU_HEAD_W, (hd + 1) * SGU_HEAD_W)
        for c in range(tm // CHUNK):
            rs = slice(c * CHUNK, (c + 1) * CHUNK)
            s = jnp.dot(w, vn[rs, cs], preferred_element_type=F32) + bias
            uu = u_ref[rs, cs].astype(F32)
            gg = gt_ref[rs, cs].astype(F32)
            o_ref[rs, cs] = (uu * s * _silu(gg)).astype(o_ref.dtype)


def _attn_kernel(q_ref, k_ref, v_ref, gt_ref, lq1_ref, lk1_ref, lq2_ref, lk2_ref, sg_ref,
                 o_ref, qzt_ref, vt_ref, s_ref, m_ref, acc_ref, fin_ref, *, tq, lam_init):
    nq = vt_ref.shape[0]

    pad = lax.broadcasted_iota(jnp.int32, (ATT_PAD_ROWS, tq), 0)
    ones_row = jnp.where(pad == 0, 1.0, 0.0).astype(BF16)
    for j in range(nq):
        vt_ref[j, :ATT_HEAD_W, :] = v_ref[j * tq:(j + 1) * tq, :].T
        vt_ref[j, ATT_HEAD_W:, :] = ones_row

    lam = (jnp.exp(jnp.sum(lq1_ref[...] * lk1_ref[...], axis=-1, keepdims=True))
           - jnp.exp(jnp.sum(lq2_ref[...] * lk2_ref[...], axis=-1, keepdims=True)) + lam_init)

    def block_rows(i):
        return pl.ds(pl.multiple_of(i * tq, tq), tq)

    def prep_q(qi):
        qt = q_ref[block_rows(qi), :].T
        sub = lax.broadcasted_iota(jnp.int32, (ATT_HEAD_W, tq), 0)
        qzt_ref[0] = jnp.where(sub < ATT_HEAD_D, qt, 0).astype(BF16)
        qzt_ref[1] = jnp.where(sub >= ATT_HEAD_D, qt, 0).astype(BF16)

    def scores(st, kb):
        s_ref[st] = jnp.dot(k_ref[block_rows(kb), :], qzt_ref[st],
                            preferred_element_type=F32)

    def softmax_pv(st, kb, masked):
        s = s_ref[st]
        if masked:
            key = lax.broadcasted_iota(jnp.int32, (tq, tq), 0)
            qry = lax.broadcasted_iota(jnp.int32, (tq, tq), 1)
            s = jnp.where(key <= qry, s, MASK_VALUE)
        m = m_ref[st]
        m_new = jnp.maximum(m, jnp.max(s, axis=0, keepdims=True))
        alpha = jnp.exp2(m - m_new)
        p = jnp.exp2(s - m_new).astype(BF16)
        acc_ref[st] = alpha * acc_ref[st] + jnp.dot(vt_ref[kb], p, preferred_element_type=F32)
        m_ref[st] = m_new

    def reset_stats():
        m_ref[...] = jnp.full(m_ref.shape, MASK_VALUE, F32)
        acc_ref[...] = jnp.zeros(acc_ref.shape, F32)

    def finish(qi):
        a1 = fin_ref[0]
        a2 = fin_ref[1]
        o1 = a1[:ATT_HEAD_W] * (1.0 / a1[ATT_HEAD_W:ATT_HEAD_W + 1])
        o2 = a2[:ATT_HEAD_W] * (1.0 / a2[ATT_HEAD_W:ATT_HEAD_W + 1])
        ot = o1 - lam * o2
        o = _rms(ot.T, sg_ref[...]) * (1.0 - lam_init)
        rows = block_rows(qi)
        o_ref[rows, :] = (o * _silu(gt_ref[rows, :].astype(F32))).astype(o_ref.dtype)

    prep_q(0)
    reset_stats()
    den = lax.broadcasted_iota(jnp.int32, fin_ref.shape, 1)
    fin_ref[...] = jnp.where(den == ATT_HEAD_W, 1.0, 0.0)
    scores(0, 0)

    def q_block(qi, c):
        def k_block(kb):
            scores(1, kb)
            softmax_pv(0, kb, False)
            scores(0, kb + 1)
            softmax_pv(1, kb, False)

        def k_pair(j, c2):
            k_block(2 * j)
            k_block(2 * j + 1)
            return c2

        def k_single(j, c2):
            k_block(qi - 1)
            return c2

        lax.fori_loop(0, qi // 2, k_pair, 0)
        lax.fori_loop(0, qi % 2, k_single, 0)
        scores(1, qi)
        softmax_pv(0, qi, True)
        prep_q(jnp.minimum(qi + 1, nq - 1))
        scores(0, 0)
        finish(jnp.maximum(qi - 1, 0))
        softmax_pv(1, qi, True)
        fin_ref[...] = acc_ref[...]
        reset_stats()
        return c

    lax.fori_loop(0, nq, q_block, 0)
    finish(nq - 1)


def _attention(z_rest, lq1, lk1, lq2, lk2, subln_g, *, nbatch, seq, tq, lam_init):
    nq = seq // tq
    const = lambda *shape: pl.BlockSpec(shape, lambda b, h: (0,) * len(shape))
    return pl.pallas_call(
        functools.partial(_attn_kernel, tq=tq, lam_init=lam_init),
        grid=(nbatch, ATT_HEADS),
        in_specs=[
            pl.BlockSpec((seq, ATT_HEAD_W), lambda b, h: (b, ATT_Q_OFF + h)),
            pl.BlockSpec((seq, ATT_HEAD_W), lambda b, h: (b, ATT_K_OFF + h)),
            pl.BlockSpec((seq, ATT_HEAD_W), lambda b, h: (b, ATT_V_OFF + h)),
            pl.BlockSpec((seq, ATT_HEAD_W), lambda b, h: (b, ATT_G_OFF + h)),
            const(1, ATT_HEAD_D),
            const(1, ATT_HEAD_D),
            const(1, ATT_HEAD_D),
            const(1, ATT_HEAD_D),
            const(1, ATT_HEAD_W),
        ],
        out_specs=pl.BlockSpec((seq, ATT_HEAD_W), lambda b, h: (b, h)),
        out_shape=jax.ShapeDtypeStruct((nbatch * seq, ATT_W), BF16),
        scratch_shapes=[pltpu.VMEM((2, ATT_HEAD_W, tq), BF16),
                        pltpu.VMEM((nq, ATT_HEAD_W + ATT_PAD_ROWS, tq), BF16),
                        pltpu.VMEM((2, tq, tq), F32),
                        pltpu.VMEM((2, 1, tq), F32),
                        pltpu.VMEM((2, ATT_HEAD_W + ATT_PAD_ROWS, tq), F32),
                        pltpu.VMEM((2, ATT_HEAD_W + ATT_PAD_ROWS, tq), F32)],
        compiler_params=pltpu.CompilerParams(
            dimension_semantics=("parallel", "parallel"),
            vmem_limit_bytes=V7X_VMEM_LIMIT_BYTES),
        name="diff_attention",
    )(z_rest, z_rest, z_rest, z_rest, lq1, lk1, lq2, lk2, subln_g)


def _outproj_kernel(ys_ref, u_ref, v_ref, gt_ref, lg_ref, lb_ref, ws_ref, bt_ref, ya_ref, w_ref,
                    x_ref, fg_ref, o_ref, yg_ref, *, tm, final_norm):
    _sgu_mix(u_ref, v_ref, gt_ref, lg_ref, lb_ref, ws_ref, bt_ref, yg_ref, tm)
    acc = jnp.dot(ys_ref[...], w_ref[:SSM_W, :], preferred_element_type=F32)
    acc = acc + jnp.dot(yg_ref[...], w_ref[SSM_W:SSM_W + SGU_W, :], preferred_element_type=F32)
    acc = acc + jnp.dot(ya_ref[...], w_ref[SSM_W + SGU_W:, :], preferred_element_type=F32)
    xn = x_ref[...] + acc
    if final_norm:
        xn = _rms(xn, fg_ref[...])
    o_ref[...] = xn


def _outproj(y_ssm, z_rest, sgu_ln_g, sgu_ln_b, sgu_w, sgu_bt, y_att, w_out, x2d, final_g, *,
             tm, final_norm):
    m, d = x2d.shape
    assert tm % CHUNK == 0
    return pl.pallas_call(
        functools.partial(_outproj_kernel, tm=tm, final_norm=final_norm),
        grid=(m // tm,),
        in_specs=[
            pl.BlockSpec((tm, SSM_W), lambda i: (i, 0)),
            pl.BlockSpec((tm, SGU_W), lambda i: (i, SGU_U_BLK)),
            pl.BlockSpec((tm, SGU_W), lambda i: (i, SGU_V_BLK)),
            pl.BlockSpec((tm, SGU_W), lambda i: (i, SGU_G_BLK)),
            _resident(1, SGU_W),
            _resident(1, SGU_W),
            _resident(SGU_HEADS, CHUNK, CHUNK),
            _resident(CHUNK, SGU_HEADS),
            pl.BlockSpec((tm, ATT_W), lambda i: (i, 0)),
            _resident(D_MIX, d),
            pl.BlockSpec((tm, d), lambda i: (i, 0)),
            _resident(1, d),
        ],
        out_specs=pl.BlockSpec((tm, d), lambda i: (i, 0)),
        out_shape=jax.ShapeDtypeStruct((m, d), F32),
        scratch_shapes=[pltpu.VMEM((tm, SGU_W), BF16)],
        compiler_params=pltpu.CompilerParams(
            dimension_semantics=("parallel",),
            vmem_limit_bytes=V7X_VMEM_LIMIT_BYTES),
        name="outproj",
    )(y_ssm, z_rest, z_rest, z_rest, sgu_ln_g, sgu_ln_b, sgu_w, sgu_bt, y_att, w_out, x2d, final_g)


def _pick(n, prefs):
    for p in prefs:
        if n % p == 0:
            return p
    raise ValueError(f"no supported tile for extent {n}")


def kernel(x, norm_g, w_in, ssm_a_re, ssm_a_im, ssm_log_step, ssm_b_re, ssm_b_im, ssm_c_re, ssm_c_im,
           ssm_d, glu_w, glu_b, sgu_ln_g, sgu_ln_b, sgu_w, sgu_b, lam_q1, lam_k1, lam_q2, lam_k2,
           attn_subln_g, w_out, final_g):
    nbatch, seq, d = x.shape
    depth = norm_g.shape[0]
    assert d == D_MODEL and seq % 512 == 0 and nbatch == 8
    m = nbatch * seq
    tm_in = _pick(m, (512, 256))
    tt = _pick(seq, (64, 32))
    tq = 512
    q_lo = SSM_COLS + 3 * SGU_W
    col_scale = jnp.ones((SSM_COLS + REST_COLS,), F32).at[q_lo:q_lo + ATT_W].set(
        ATT_HEAD_D ** -0.5 * LOG2_E)

    x2d = x.reshape(m, d).astype(F32)
    fg = final_g.reshape(1, d).astype(F32)
    for l in range(depth):
        g = norm_g[l].reshape(1, d).astype(F32)
        w_l = (w_in[l].astype(F32) * col_scale).astype(BF16)
        lam_init = 0.8 - 0.6 * math.exp(-0.3 * l)

        z_rest = _inproj(x2d, g, w_l[:, SSM_COLS:], tm=tm_in)

        ssm_p = _ssm_params(ssm_a_re[l], ssm_a_im[l], ssm_log_step[l], ssm_b_re[l], ssm_b_im[l],
                            ssm_c_re[l], ssm_c_im[l], nbatch)
        y_ssm_tb = _ssm_branch(x2d.reshape(nbatch, seq, d), g, w_l[:, :SSM_COLS], *ssm_p,
                               ssm_d[l].reshape(1, SSM_W).astype(F32), glu_w[l].astype(BF16),
                               glu_b[l].reshape(1, SSM_W).astype(F32), tt=tt)
        y_ssm = y_ssm_tb.reshape(seq, nbatch, SSM_W).swapaxes(0, 1).reshape(m, SSM_W)

        y_att = _attention(z_rest, lam_q1[l].reshape(1, -1).astype(F32), lam_k1[l].reshape(1, -1).astype(F32),
                           lam_q2[l].reshape(1, -1).astype(F32), lam_k2[l].reshape(1, -1).astype(F32),
                           attn_subln_g[l].reshape(1, ATT_HEAD_W).astype(F32),
                           nbatch=nbatch, seq=seq, tq=tq, lam_init=lam_init)

        x2d = _outproj(y_ssm, z_rest, sgu_ln_g[l].reshape(1, SGU_W).astype(F32),
                       sgu_ln_b[l].reshape(1, SGU_W).astype(F32), sgu_w[l].astype(F32),
                       sgu_b[l].astype(F32).T, y_att, w_out[l].astype(BF16), x2d, fg,
                       tm=512, final_norm=(l == depth - 1))
    return x2d.reshape(nbatch, seq, d).astype(x.dtype)
```

```python
import functools
import math

import jax
import jax.numpy as jnp
from jax import lax
from jax.experimental import pallas as pl
from jax.experimental.pallas import tpu as pltpu

F32 = jnp.float32
BF16 = jnp.bfloat16

D_MODEL = 1024
D_MIX = 2 * D_MODEL
SSM_W = D_MIX // 4
SSM_GROUP_CH = 16
SSM_GROUPS = SSM_W // SSM_GROUP_CH
SSM_STATE = 64
SSM_CH = SSM_GROUPS * SSM_STATE
SSM_HALF_IN = SSM_W // 2
SSM_HALF_CH = SSM_CH // 2
SGU_W = D_MIX // 4
CHUNK = 128
SGU_HEADS = 4
SGU_HEAD_W = SGU_W // SGU_HEADS
ATT_W = D_MIX // 2
ATT_HEAD_D = 64
ATT_HEADS = ATT_W // (2 * ATT_HEAD_D)
ATT_HEAD_W = 2 * ATT_HEAD_D
ATT_PAD_ROWS = 16
NORM_EPS = 1e-6
LN_EPS = 1e-5

SSM_COLS = 2 * SSM_W
REST_COLS = 3 * SGU_W + 4 * ATT_W
SGU_U_BLK, SGU_V_BLK, SGU_G_BLK = 0, 1, 2
ATT_Q_OFF = 3 * SGU_W // ATT_HEAD_W
ATT_K_OFF = ATT_Q_OFF + ATT_HEADS
ATT_V_OFF = ATT_K_OFF + ATT_HEADS
ATT_G_OFF = ATT_V_OFF + ATT_HEADS

V7X_VMEM_LIMIT_BYTES = 56 * 1024 * 1024
MXU_COL_CHUNK = 1024
LOG2_E = math.log2(math.e)
MASK_VALUE = -1e30


def _rms(x, g):
    ms = jnp.mean(x * x, axis=-1, keepdims=True)
    return x * lax.rsqrt(ms + NORM_EPS) * g


def _silu(x):
    return x * (1.0 / (1.0 + jnp.exp(-x)))


def _sigmoid(x):
    return 1.0 / (1.0 + jnp.exp(-x))


def _gelu_exact(x):
    return 0.5 * x * (1.0 + lax.erf(x * math.sqrt(0.5)))


def _inproj_kernel(x_ref, g_ref, w_ref, o_ref, *, col_chunks):
    h = _rms(x_ref[...], g_ref[...]).astype(BF16)
    c0 = 0
    for cw in col_chunks:
        o_ref[:, c0:c0 + cw] = jnp.dot(h, w_ref[:, c0:c0 + cw],
                                       preferred_element_type=F32).astype(o_ref.dtype)
        c0 += cw


def _resident(*shape):
    return pl.BlockSpec(shape, lambda *_: (0,) * len(shape), pipeline_mode=pl.Buffered(1))


def _inproj(x2d, g, w, *, tm):
    m, d = x2d.shape
    n = w.shape[1]
    col_chunks = (MXU_COL_CHUNK,) * (n // MXU_COL_CHUNK)
    if n % MXU_COL_CHUNK:
        col_chunks += (n % MXU_COL_CHUNK,)
    return pl.pallas_call(
        functools.partial(_inproj_kernel, col_chunks=col_chunks),
        grid=(m // tm,),
        in_specs=[
            pl.BlockSpec((tm, d), lambda i: (i, 0)),
            _resident(1, d),
            _resident(d, n),
        ],
        out_specs=pl.BlockSpec((tm, n), lambda i: (i, 0)),
        out_shape=jax.ShapeDtypeStruct((m, n), BF16),
        compiler_params=pltpu.CompilerParams(
            dimension_semantics=("parallel",),
            vmem_limit_bytes=V7X_VMEM_LIMIT_BYTES),
        name="inproj",
    )(x2d, g, w)


def _ssm_kernel(x_hbm, g_ref, w_ref, are_ref, aim_ref, bbd_ref, cre_ref, ncim_ref, d_ref,
                gw_ref, gb_ref, o_ref, xbuf, sem, z_ref, bre_ref, bim_ref, carry_ref,
                *, tt, nb, nblk):
    j = pl.program_id(0)
    r = tt * nb

    def x_copies(slot, blk):
        return [pltpu.make_async_copy(x_hbm.at[b, pl.ds(blk * tt, tt), :], xbuf.at[slot, :, b, :],
                                      sem.at[slot, b]) for b in range(nb)]

    def fetch(slot, blk):
        for c in x_copies(slot, blk):
            c.start()

    def await_x(slot, blk):
        for c in x_copies(slot, blk):
            c.wait()

    def project(slot):
        x = xbuf[slot].reshape(r, x_hbm.shape[-1])
        h = _rms(x, g_ref[...]).astype(BF16)
        z = jnp.dot(h, w_ref[...], preferred_element_type=F32)
        z_ref[slot] = z
        ub = z[:, :SSM_W].astype(BF16)
        for hf in range(2):
            bu = jnp.dot(ub[:, hf * SSM_HALF_IN:(hf + 1) * SSM_HALF_IN], bbd_ref[hf],
                         preferred_element_type=F32)
            bre_ref[slot, :, hf * SSM_HALF_CH:(hf + 1) * SSM_HALF_CH] = bu[:, :SSM_HALF_CH]
            bim_ref[slot, :, hf * SSM_HALF_CH:(hf + 1) * SSM_HALF_CH] = bu[:, SSM_HALF_CH:]

    def scan(slot):
        sr = carry_ref[0]
        si = carry_ref[1]
        ar = are_ref[...]
        ai = aim_ref[...]
        for t in range(tt):
            rows = slice(t * nb, (t + 1) * nb)
            nsr = ar * sr - ai * si + bre_ref[slot, rows, :]
            nsi = ar * si + ai * sr + bim_ref[slot, rows, :]
            bre_ref[slot, rows, :] = nsr
            bim_ref[slot, rows, :] = nsi
            sr, si = nsr, nsi
        carry_ref[0] = sr
        carry_ref[1] = si

    def emit(slot):
        u = z_ref[slot, :, :SSM_W]
        gate = z_ref[slot, :, SSM_W:]
        ys = []
        for hf in range(2):
            cols = slice(hf * SSM_HALF_CH, (hf + 1) * SSM_HALF_CH)
            yh = jnp.dot(bre_ref[slot, :, cols].astype(BF16), cre_ref[hf], preferred_element_type=F32)
            yh = yh + jnp.dot(bim_ref[slot, :, cols].astype(BF16), ncim_ref[hf],
                              preferred_element_type=F32)
            ys.append(yh)
        y = jnp.concatenate(ys, axis=-1) + d_ref[...] * u
        y = _gelu_exact(y)
        y = y * _sigmoid(jnp.dot(y.astype(BF16), gw_ref[...], preferred_element_type=F32) + gb_ref[...])
        o_ref[slot * r:(slot + 1) * r, :] = (y * _silu(gate)).astype(o_ref.dtype)

    @pl.when(j == 0)
    def _():
        carry_ref[...] = jnp.zeros_like(carry_ref)
        fetch(0, 0)
        await_x(0, 0)
        project(0)
        fetch(1, 1)

    has_next = 2 * j + 2 < nblk

    @pl.when(has_next)
    def _():
        fetch(0, 2 * j + 2)

    await_x(1, 2 * j + 1)
    project(1)
    scan(0)
    emit(0)

    @pl.when(has_next)
    def _():
        fetch(1, 2 * j + 3)
        await_x(0, 2 * j + 2)

    project(0)
    scan(1)
    emit(1)


def _ssm_branch(x3d, g, w_ssm, are8, aim8, bbd, cre_bd, ncim_bd, d, glu_w, glu_b, *, tt):
    nb, seq, dm = x3d.shape
    r = tt * nb
    nblk = seq // tt
    assert nblk % 2 == 0
    return pl.pallas_call(
        functools.partial(_ssm_kernel, tt=tt, nb=nb, nblk=nblk),
        grid=(nblk // 2,),
        in_specs=[
            pl.BlockSpec(memory_space=pl.ANY),
            _resident(1, dm),
            _resident(dm, SSM_COLS),
            _resident(nb, SSM_CH),
            _resident(nb, SSM_CH),
            _resident(2, SSM_HALF_IN, 2 * SSM_HALF_CH),
            _resident(2, SSM_HALF_CH, SSM_HALF_IN),
            _resident(2, SSM_HALF_CH, SSM_HALF_IN),
            _resident(1, SSM_W),
            _resident(SSM_W, SSM_W),
            _resident(1, SSM_W),
        ],
        out_specs=pl.BlockSpec((2 * r, SSM_W), lambda i: (i, 0)),
        out_shape=jax.ShapeDtypeStruct((seq * nb, SSM_W), BF16),
        scratch_shapes=[
            pltpu.VMEM((2, tt, nb, dm), F32),
            pltpu.SemaphoreType.DMA((2, nb)),
            pltpu.VMEM((2, r, SSM_COLS), F32),
            pltpu.VMEM((2, r, SSM_CH), F32),
            pltpu.VMEM((2, r, SSM_CH), F32),
            pltpu.VMEM((2, nb, SSM_CH), F32),
        ],
        compiler_params=pltpu.CompilerParams(
            dimension_semantics=("arbitrary",),
            vmem_limit_bytes=V7X_VMEM_LIMIT_BYTES),
        name="ssm_branch",
    )(x3d, g, w_ssm, are8, aim8, bbd, cre_bd, ncim_bd, d, glu_w, glu_b)


def _ssm_params(a_re, a_im, log_step, b_re, b_im, c_re, c_im, nb):
    lr = a_re.astype(F32)
    li = a_im.astype(F32)
    step = jnp.exp(log_step.astype(F32))[:, None]
    mag = jnp.exp(step * lr)
    ang = step * li
    ab_re = mag * jnp.cos(ang)
    ab_im = mag * jnp.sin(ang)
    den = lr * lr + li * li
    nr = ab_re - 1.0
    ni = ab_im
    co_re = (nr * lr + ni * li) / den
    co_im = (ni * lr - nr * li) / den
    br = b_re.astype(F32)
    bi = b_im.astype(F32)
    bb_re = co_re[..., None] * br - co_im[..., None] * bi
    bb_im = co_re[..., None] * bi + co_im[..., None] * br
    gh = SSM_GROUPS // 2
    eye = jnp.eye(gh, dtype=F32)

    def b_blockdiag(bb):
        t = bb.reshape(2, gh, SSM_STATE, SSM_GROUP_CH)
        t = jnp.einsum('fgph,gk->fghkp', t, eye)
        return t.reshape(2, gh * SSM_GROUP_CH, gh * SSM_STATE)

    def c_blockdiag(c):
        t = c.astype(F32).reshape(2, gh, SSM_GROUP_CH, SSM_STATE)
        t = jnp.einsum('fghp,gk->fgpkh', t, eye)
        return t.reshape(2, gh * SSM_STATE, gh * SSM_GROUP_CH)

    bbd = jnp.concatenate([b_blockdiag(bb_re), b_blockdiag(bb_im)], axis=-1).astype(BF16)
    cre_bd = c_blockdiag(c_re).astype(BF16)
    ncim_bd = (-c_blockdiag(c_im)).astype(BF16)
    are8 = jnp.broadcast_to(ab_re.reshape(1, SSM_CH), (nb, SSM_CH))
    aim8 = jnp.broadcast_to(ab_im.reshape(1, SSM_CH), (nb, SSM_CH))
    return are8, aim8, bbd, cre_bd, ncim_bd


def _sgu_mix(u_ref, v_ref, gt_ref, lg_ref, lb_ref, w_ref, bt_ref, o_ref, tm):
    v = v_ref[...].astype(F32)
    mu = jnp.mean(v, axis=-1, keepdims=True)
    vc = v - mu
    var = jnp.mean(vc * vc, axis=-1, keepdims=True)
    vn = (vc * lax.rsqrt(var + LN_EPS) * lg_ref[...] + lb_ref[...]).astype(BF16)
    row = lax.broadcasted_iota(jnp.int32, (CHUNK, CHUNK), 0)
    col = lax.broadcasted_iota(jnp.int32, (CHUNK, CHUNK), 1)
    for hd in range(SGU_HEADS):
        w = jnp.where(col <= row, w_ref[hd], 0.0).astype(BF16)
        bias = bt_ref[:, hd:hd + 1]
        cs = slice(hd * SGU_HEAD_W, (hd + 1) * SGU_HEAD_W)
        for c in range(tm // CHUNK):
            rs = slice(c * CHUNK, (c + 1) * CHUNK)
            s = jnp.dot(w, vn[rs, cs], preferred_element_type=F32) + bias
            uu = u_ref[rs, cs].astype(F32)
            gg = gt_ref[rs, cs].astype(F32)
            o_ref[rs, cs] = (uu * s * _silu(gg)).astype(o_ref.dtype)


def _attn_kernel(q_ref, k_ref, v_ref, gt_ref, lq1_ref, lk1_ref, lq2_ref, lk2_ref, sg_ref,
                 o_ref, qzt_ref, vt_ref, s_ref, m_ref, acc_ref, fin_ref, *, tq, tk, lam_init):
    nkb = vt_ref.shape[0]
    nq = nkb // 2

    pad = lax.broadcasted_iota(jnp.int32, (ATT_PAD_ROWS, tk), 0)
    ones_row = jnp.where(pad == 0, 1.0, 0.0).astype(BF16)
    for j in range(nkb):
        vt_ref[j, :ATT_HEAD_W, :] = v_ref[j * tk:(j + 1) * tk, :].T
        vt_ref[j, ATT_HEAD_W:, :] = ones_row

    lam = (jnp.exp(jnp.sum(lq1_ref[...] * lk1_ref[...], axis=-1, keepdims=True))
           - jnp.exp(jnp.sum(lq2_ref[...] * lk2_ref[...], axis=-1, keepdims=True)) + lam_init)

    def q_rows(qi):
        return pl.ds(pl.multiple_of(qi * tq, tq), tq)

    def k_rows(kb):
        return pl.ds(pl.multiple_of(kb * tk, tk), tk)

    def prep_q(qi):
        qt = q_ref[q_rows(qi), :].T
        sub = lax.broadcasted_iota(jnp.int32, (ATT_HEAD_W, tq), 0)
        qzt_ref[0] = jnp.where(sub < ATT_HEAD_D, qt, 0).astype(BF16)
        qzt_ref[1] = jnp.where(sub >= ATT_HEAD_D, qt, 0).astype(BF16)

    def scores(st, kb, lo=0):
        s_ref[st, :, lo:] = jnp.dot(k_ref[k_rows(kb), :], qzt_ref[st, :, lo:],
                                    preferred_element_type=F32)

    def softmax_pv(st, kb, masked, lo=0):
        s = s_ref[st, :, lo:]
        if masked:
            key = lax.broadcasted_iota(jnp.int32, (tk, tq - lo), 0)
            qry = lax.broadcasted_iota(jnp.int32, (tk, tq - lo), 1)
            s = jnp.where(key <= qry, s, MASK_VALUE)
        m = m_ref[st, :, lo:]
        m_new = jnp.maximum(m, jnp.max(s, axis=0, keepdims=True))
        alpha = jnp.exp2(m - m_new)
        p = jnp.exp2(s - m_new).astype(BF16)
        acc_ref[st, :, lo:] = (alpha * acc_ref[st, :, lo:]
                               + jnp.dot(vt_ref[kb], p, preferred_element_type=F32))
        m_ref[st, :, lo:] = m_new

    def reset_stats():
        m_ref[...] = jnp.full(m_ref.shape, MASK_VALUE, F32)
        acc_ref[...] = jnp.zeros(acc_ref.shape, F32)

    def finish(qi):
        a1 = fin_ref[0]
        a2 = fin_ref[1]
        o1 = a1[:ATT_HEAD_W] * (1.0 / a1[ATT_HEAD_W:ATT_HEAD_W + 1])
        o2 = a2[:ATT_HEAD_W] * (1.0 / a2[ATT_HEAD_W:ATT_HEAD_W + 1])
        ot = o1 - lam * o2
        o = _rms(ot.T, sg_ref[...]) * (1.0 - lam_init)
        rows = q_rows(qi)
        o_ref[rows, :] = (o * _silu(gt_ref[rows, :].astype(F32))).astype(o_ref.dtype)

    prep_q(0)
    reset_stats()
    den = lax.broadcasted_iota(jnp.int32, fin_ref.shape, 1)
    fin_ref[...] = jnp.where(den == ATT_HEAD_W, 1.0, 0.0)
    scores(0, 0)

    def q_block(qi, c):
        def k_block(kb):
            scores(1, kb)
            softmax_pv(0, kb, False)
            scores(0, kb + 1)
            softmax_pv(1, kb, False)

        def k_pair(j, c2):
            k_block(2 * j)
            k_block(2 * j + 1)
            return c2

        lax.fori_loop(0, qi, k_pair, 0)
        da = 2 * qi
        scores(1, da)
        softmax_pv(0, da, True)
        scores(0, da + 1, tk)
        softmax_pv(1, da, True)
        scores(1, da + 1, tk)
        softmax_pv(0, da + 1, True, tk)
        prep_q(jnp.minimum(qi + 1, nq - 1))
        scores(0, 0)
        finish(jnp.maximum(qi - 1, 0))
        softmax_pv(1, da + 1, True, tk)
        fin_ref[...] = acc_ref[...]
        reset_stats()
        return c

    lax.fori_loop(0, nq, q_block, 0)
    finish(nq - 1)


def _attention(z_rest, lq1, lk1, lq2, lk2, subln_g, *, nbatch, seq, tk, lam_init):
    tq = 2 * tk
    const = lambda *shape: pl.BlockSpec(shape, lambda b, h: (0,) * len(shape))
    return pl.pallas_call(
        functools.partial(_attn_kernel, tq=tq, tk=tk, lam_init=lam_init),
        grid=(nbatch, ATT_HEADS),
        in_specs=[
            pl.BlockSpec((seq, ATT_HEAD_W), lambda b, h: (b, ATT_Q_OFF + h)),
            pl.BlockSpec((seq, ATT_HEAD_W), lambda b, h: (b, ATT_K_OFF + h)),
            pl.BlockSpec((seq, ATT_HEAD_W), lambda b, h: (b, ATT_V_OFF + h)),
            pl.BlockSpec((seq, ATT_HEAD_W), lambda b, h: (b, ATT_G_OFF + h)),
            const(1, ATT_HEAD_D),
            const(1, ATT_HEAD_D),
            const(1, ATT_HEAD_D),
            const(1, ATT_HEAD_D),
            const(1, ATT_HEAD_W),
        ],
        out_specs=pl.BlockSpec((seq, ATT_HEAD_W), lambda b, h: (b, h)),
        out_shape=jax.ShapeDtypeStruct((nbatch * seq, ATT_W), BF16),
        scratch_shapes=[pltpu.VMEM((2, ATT_HEAD_W, tq), BF16),
                        pltpu.VMEM((seq // tk, ATT_HEAD_W + ATT_PAD_ROWS, tk), BF16),
                        pltpu.VMEM((2, tk, tq), F32),
                        pltpu.VMEM((2, 1, tq), F32),
                        pltpu.VMEM((2, ATT_HEAD_W + ATT_PAD_ROWS, tq), F32),
                        pltpu.VMEM((2, ATT_HEAD_W + ATT_PAD_ROWS, tq), F32)],
        compiler_params=pltpu.CompilerParams(
            dimension_semantics=("parallel", "parallel"),
            vmem_limit_bytes=V7X_VMEM_LIMIT_BYTES),
        name="diff_attention",
    )(z_rest, z_rest, z_rest, z_rest, lq1, lk1, lq2, lk2, subln_g)


def _outproj_kernel(ys_ref, u_ref, v_ref, gt_ref, lg_ref, lb_ref, ws_ref, bt_ref, ya_ref, w_ref,
                    x_ref, fg_ref, o_ref, yg_ref, *, tm, final_norm):
    _sgu_mix(u_ref, v_ref, gt_ref, lg_ref, lb_ref, ws_ref, bt_ref, yg_ref, tm)
    acc = jnp.dot(ys_ref[...], w_ref[:SSM_W, :], preferred_element_type=F32)
    acc = acc + jnp.dot(yg_ref[...], w_ref[SSM_W:SSM_W + SGU_W, :], preferred_element_type=F32)
    acc = acc + jnp.dot(ya_ref[...], w_ref[SSM_W + SGU_W:, :], preferred_element_type=F32)
    xn = x_ref[...] + acc
    if final_norm:
        xn = _rms(xn, fg_ref[...])
    o_ref[...] = xn


def _outproj(y_ssm, z_rest, sgu_ln_g, sgu_ln_b, sgu_w, sgu_bt, y_att, w_out, x2d, final_g, *,
             tm, final_norm):
    m, d = x2d.shape
    assert tm % CHUNK == 0
    return pl.pallas_call(
        functools.partial(_outproj_kernel, tm=tm, final_norm=final_norm),
        grid=(m // tm,),
        in_specs=[
            pl.BlockSpec((tm, SSM_W), lambda i: (i, 0)),
            pl.BlockSpec((tm, SGU_W), lambda i: (i, SGU_U_BLK)),
            pl.BlockSpec((tm, SGU_W), lambda i: (i, SGU_V_BLK)),
            pl.BlockSpec((tm, SGU_W), lambda i: (i, SGU_G_BLK)),
            _resident(1, SGU_W),
            _resident(1, SGU_W),
            _resident(SGU_HEADS, CHUNK, CHUNK),
            _resident(CHUNK, SGU_HEADS),
            pl.BlockSpec((tm, ATT_W), lambda i: (i, 0)),
            _resident(D_MIX, d),
            pl.BlockSpec((tm, d), lambda i: (i, 0)),
            _resident(1, d),
        ],
        out_specs=pl.BlockSpec((tm, d), lambda i: (i, 0)),
        out_shape=jax.ShapeDtypeStruct((m, d), F32),
        scratch_shapes=[pltpu.VMEM((tm, SGU_W), BF16)],
        compiler_params=pltpu.CompilerParams(
            dimension_semantics=("parallel",),
            vmem_limit_bytes=V7X_VMEM_LIMIT_BYTES),
        name="outproj",
    )(y_ssm, z_rest, z_rest, z_rest, sgu_ln_g, sgu_ln_b, sgu_w, sgu_bt, y_att, w_out, x2d, final_g)


def _pick(n, prefs):
    for p in prefs:
        if n % p == 0:
            return p
    raise ValueError(f"no supported tile for extent {n}")


def kernel(x, norm_g, w_in, ssm_a_re, ssm_a_im, ssm_log_step, ssm_b_re, ssm_b_im, ssm_c_re, ssm_c_im,
           ssm_d, glu_w, glu_b, sgu_ln_g, sgu_ln_b, sgu_w, sgu_b, lam_q1, lam_k1, lam_q2, lam_k2,
           attn_subln_g, w_out, final_g):
    nbatch, seq, d = x.shape
    depth = norm_g.shape[0]
    assert d == D_MODEL and seq % 1024 == 0 and nbatch == 8
    m = nbatch * seq
    tm_in = _pick(m, (1024, 512, 256))
    tm_out = _pick(m, (1024, 512, 256))
    tt = _pick(seq, (64, 32))
    tk_att = 512
    q_lo = 3 * SGU_W
    rest_scale = jnp.ones((REST_COLS,), F32).at[q_lo:q_lo + ATT_W].set(ATT_HEAD_D ** -0.5 * LOG2_E)

    x2d = x.reshape(m, d).astype(F32)
    fg = final_g.reshape(1, d).astype(F32)
    for l in range(depth):
        g = norm_g[l].reshape(1, d).astype(F32)
        w_rest = (w_in[l, :, SSM_COLS:].astype(F32) * rest_scale).astype(BF16)
        w_ssm = w_in[l, :, :SSM_COLS].astype(BF16)
        lam_init = 0.8 - 0.6 * math.exp(-0.3 * l)

        z_rest = _inproj(x2d, g, w_rest, tm=tm_in)

        ssm_p = _ssm_params(ssm_a_re[l], ssm_a_im[l], ssm_log_step[l], ssm_b_re[l], ssm_b_im[l],
                            ssm_c_re[l], ssm_c_im[l], nbatch)
        y_ssm_tb = _ssm_branch(x2d.reshape(nbatch, seq, d), g, w_ssm, *ssm_p,
                               ssm_d[l].reshape(1, SSM_W).astype(F32), glu_w[l].astype(BF16),
                               glu_b[l].reshape(1, SSM_W).astype(F32), tt=tt)
        y_ssm = y_ssm_tb.reshape(seq, nbatch, SSM_W).swapaxes(0, 1).reshape(m, SSM_W)

        y_att = _attention(z_rest, lam_q1[l].reshape(1, -1).astype(F32), lam_k1[l].reshape(1, -1).astype(F32),
                           lam_q2[l].reshape(1, -1).astype(F32), lam_k2[l].reshape(1, -1).astype(F32),
                           attn_subln_g[l].reshape(1, ATT_HEAD_W).astype(F32),
                           nbatch=nbatch, seq=seq, tk=tk_att, lam_init=lam_init)

        x2d = _outproj(y_ssm, z_rest, sgu_ln_g[l].reshape(1, SGU_W).astype(F32),
                       sgu_ln_b[l].reshape(1, SGU_W).astype(F32), sgu_w[l].astype(F32),
                       sgu_b[l].astype(F32).T, y_att, w_out[l].astype(BF16), x2d, fg,
                       tm=tm_out, final_norm=(l == depth - 1))
    return x2d.reshape(nbatch, seq, d).astype(x.dtype)
```

```python
import functools
import math

import jax
import jax.numpy as jnp
from jax import lax
from jax.experimental import pallas as pl
from jax.experimental.pallas import tpu as pltpu

F32 = jnp.float32
BF16 = jnp.bfloat16

D_MODEL = 1024
D_MIX = 2 * D_MODEL
SSM_W = D_MIX // 4
SSM_GROUP_CH = 16
SSM_GROUPS = SSM_W // SSM_GROUP_CH
SSM_STATE = 64
SSM_CH = SSM_GROUPS * SSM_STATE
SSM_HALF_IN = SSM_W // 2
SSM_HALF_CH = SSM_CH // 2
SGU_W = D_MIX // 4
CHUNK = 128
SGU_HEADS = 4
SGU_HEAD_W = SGU_W // SGU_HEADS
ATT_W = D_MIX // 2
ATT_HEAD_D = 64
ATT_HEADS = ATT_W // (2 * ATT_HEAD_D)
ATT_HEAD_W = 2 * ATT_HEAD_D
ATT_PAD_ROWS = 16
NORM_EPS = 1e-6
LN_EPS = 1e-5

SSM_COLS = 2 * SSM_W
REST_COLS = 3 * SGU_W + 4 * ATT_W
SGU_U_BLK, SGU_V_BLK, SGU_G_BLK = 0, 1, 2
ATT_Q_OFF = 3 * SGU_W // ATT_HEAD_W
ATT_K_OFF = ATT_Q_OFF + ATT_HEADS
ATT_V_OFF = ATT_K_OFF + ATT_HEADS
ATT_G_OFF = ATT_V_OFF + ATT_HEADS

V7X_VMEM_LIMIT_BYTES = 56 * 1024 * 1024
MXU_COL_CHUNK = 1024
LOG2_E = math.log2(math.e)
MASK_VALUE = -1e30


def _rms(x, g):
    ms = jnp.mean(x * x, axis=-1, keepdims=True)
    return x * lax.rsqrt(ms + NORM_EPS) * g


def _silu(x):
    return x * (1.0 / (1.0 + jnp.exp(-x)))


def _sigmoid(x):
    return 1.0 / (1.0 + jnp.exp(-x))


def _gelu_exact(x):
    return 0.5 * x * (1.0 + lax.erf(x * math.sqrt(0.5)))


def _inproj_kernel(x_ref, g_ref, w_ref, o_ref, *, col_chunks):
    h = _rms(x_ref[...], g_ref[...]).astype(BF16)
    c0 = 0
    for cw in col_chunks:
        o_ref[:, c0:c0 + cw] = jnp.dot(h, w_ref[:, c0:c0 + cw],
                                       preferred_element_type=F32).astype(o_ref.dtype)
        c0 += cw


def _resident(*shape):
    return pl.BlockSpec(shape, lambda *_: (0,) * len(shape), pipeline_mode=pl.Buffered(1))


def _inproj(x2d, g, w, *, tm):
    m, d = x2d.shape
    n = w.shape[1]
    col_chunks = (MXU_COL_CHUNK,) * (n // MXU_COL_CHUNK)
    if n % MXU_COL_CHUNK:
        col_chunks += (n % MXU_COL_CHUNK,)
    return pl.pallas_call(
        functools.partial(_inproj_kernel, col_chunks=col_chunks),
        grid=(m // tm,),
        in_specs=[
            pl.BlockSpec((tm, d), lambda i: (i, 0)),
            _resident(1, d),
            _resident(d, n),
        ],
        out_specs=pl.BlockSpec((tm, n), lambda i: (i, 0)),
        out_shape=jax.ShapeDtypeStruct((m, n), BF16),
        compiler_params=pltpu.CompilerParams(
            dimension_semantics=("parallel",),
            vmem_limit_bytes=V7X_VMEM_LIMIT_BYTES),
        name="inproj",
    )(x2d, g, w)


def _ssm_kernel(x_hbm, g_ref, w_ref, are_ref, aim_ref, bbd_ref, cre_ref, ncim_ref, d_ref,
                gw_ref, gb_ref, o_ref, xbuf, sem, z_ref, bre_ref, bim_ref, carry_ref,
                *, tt, nb, nblk):
    j = pl.program_id(0)
    r = tt * nb

    def x_copies(slot, blk):
        return [pltpu.make_async_copy(x_hbm.at[b, pl.ds(blk * tt, tt), :], xbuf.at[slot, :, b, :],
                                      sem.at[slot, b]) for b in range(nb)]

    def fetch(slot, blk):
        for c in x_copies(slot, blk):
            c.start()

    def await_x(slot, blk):
        for c in x_copies(slot, blk):
            c.wait()

    def project(slot):
        x = xbuf[slot].reshape(r, x_hbm.shape[-1])
        h = _rms(x, g_ref[...]).astype(BF16)
        z = jnp.dot(h, w_ref[...], preferred_element_type=F32)
        z_ref[slot] = z
        ub = z[:, :SSM_W].astype(BF16)
        for hf in range(2):
            bu = jnp.dot(ub[:, hf * SSM_HALF_IN:(hf + 1) * SSM_HALF_IN], bbd_ref[hf],
                         preferred_element_type=F32)
            bre_ref[slot, :, hf * SSM_HALF_CH:(hf + 1) * SSM_HALF_CH] = bu[:, :SSM_HALF_CH]
            bim_ref[slot, :, hf * SSM_HALF_CH:(hf + 1) * SSM_HALF_CH] = bu[:, SSM_HALF_CH:]

    def scan(slot):
        sr = carry_ref[0]
        si = carry_ref[1]
        ar = are_ref[...]
        ai = aim_ref[...]
        for t in range(tt):
            rows = slice(t * nb, (t + 1) * nb)
            nsr = ar * sr - ai * si + bre_ref[slot, rows, :]
            nsi = ar * si + ai * sr + bim_ref[slot, rows, :]
            bre_ref[slot, rows, :] = nsr
            bim_ref[slot, rows, :] = nsi
            sr, si = nsr, nsi
        carry_ref[0] = sr
        carry_ref[1] = si

    def emit(slot):
        u = z_ref[slot, :, :SSM_W]
        gate = z_ref[slot, :, SSM_W:]
        ys = []
        for hf in range(2):
            cols = slice(hf * SSM_HALF_CH, (hf + 1) * SSM_HALF_CH)
            yh = jnp.dot(bre_ref[slot, :, cols].astype(BF16), cre_ref[hf], preferred_element_type=F32)
            yh = yh + jnp.dot(bim_ref[slot, :, cols].astype(BF16), ncim_ref[hf],
                              preferred_element_type=F32)
            ys.append(yh)
        y = jnp.concatenate(ys, axis=-1) + d_ref[...] * u
        y = _gelu_exact(y)
        y = y * _sigmoid(jnp.dot(y.astype(BF16), gw_ref[...], preferred_element_type=F32) + gb_ref[...])
        o_ref[slot * r:(slot + 1) * r, :] = (y * _silu(gate)).astype(o_ref.dtype)

    @pl.when(j == 0)
    def _():
        carry_ref[...] = jnp.zeros_like(carry_ref)
        fetch(0, 0)
        await_x(0, 0)
        project(0)
        fetch(1, 1)

    has_next = 2 * j + 2 < nblk

    @pl.when(has_next)
    def _():
        fetch(0, 2 * j + 2)

    await_x(1, 2 * j + 1)
    project(1)
    scan(0)
    emit(0)

    @pl.when(has_next)
    def _():
        fetch(1, 2 * j + 3)
        await_x(0, 2 * j + 2)

    project(0)
    scan(1)
    emit(1)


def _ssm_branch(x3d, g, w_ssm, are8, aim8, bbd, cre_bd, ncim_bd, d, glu_w, glu_b, *, tt):
    nb, seq, dm = x3d.shape
    r = tt * nb
    nblk = seq // tt
    assert nblk % 2 == 0
    return pl.pallas_call(
        functools.partial(_ssm_kernel, tt=tt, nb=nb, nblk=nblk),
        grid=(nblk // 2,),
        in_specs=[
            pl.BlockSpec(memory_space=pl.ANY),
            _resident(1, dm),
            _resident(dm, SSM_COLS),
            _resident(nb, SSM_CH),
            _resident(nb, SSM_CH),
            _resident(2, SSM_HALF_IN, 2 * SSM_HALF_CH),
            _resident(2, SSM_HALF_CH, SSM_HALF_IN),
            _resident(2, SSM_HALF_CH, SSM_HALF_IN),
            _resident(1, SSM_W),
            _resident(SSM_W, SSM_W),
            _resident(1, SSM_W),
        ],
        out_specs=pl.BlockSpec((2 * r, SSM_W), lambda i: (i, 0)),
        out_shape=jax.ShapeDtypeStruct((seq * nb, SSM_W), BF16),
        scratch_shapes=[
            pltpu.VMEM((2, tt, nb, dm), F32),
            pltpu.SemaphoreType.DMA((2, nb)),
            pltpu.VMEM((2, r, SSM_COLS), F32),
            pltpu.VMEM((2, r, SSM_CH), F32),
            pltpu.VMEM((2, r, SSM_CH), F32),
            pltpu.VMEM((2, nb, SSM_CH), F32),
        ],
        compiler_params=pltpu.CompilerParams(
            dimension_semantics=("arbitrary",),
            vmem_limit_bytes=V7X_VMEM_LIMIT_BYTES),
        name="ssm_branch",
    )(x3d, g, w_ssm, are8, aim8, bbd, cre_bd, ncim_bd, d, glu_w, glu_b)


def _ssm_zoh_kernel(lr_ref, li_ref, ls_ref, br_ref, bi_ref, are_ref, aim_ref, bbr_ref, bbi_ref):
    lr = lr_ref[...]
    li = li_ref[...]
    step = jnp.exp(ls_ref[...])
    mag = jnp.exp(step * lr)
    ang = step * li
    ab_re = mag * jnp.cos(ang)
    ab_im = mag * jnp.sin(ang)
    den = lr * lr + li * li
    nr = ab_re - 1.0
    ni = ab_im
    co_re = (nr * lr + ni * li) / den
    co_im = (ni * lr - nr * li) / den
    br = br_ref[...]
    bi = bi_ref[...]
    are_ref[...] = ab_re
    aim_ref[...] = ab_im
    bbr_ref[...] = co_re * br - co_im * bi
    bbi_ref[...] = co_re * bi + co_im * br


def _ssm_params(a_re, a_im, log_step, b_re, b_im, c_re, c_im, nb):
    flat = SSM_STATE * SSM_GROUP_CH
    rep = lambda a: jnp.repeat(a.astype(F32), SSM_GROUP_CH, axis=-1)
    out = jax.ShapeDtypeStruct((SSM_GROUPS, flat), F32)
    ab_re, ab_im, bb_re, bb_im = pl.pallas_call(
        _ssm_zoh_kernel, out_shape=(out, out, out, out), name="ssm_zoh",
    )(rep(a_re), rep(a_im),
      jnp.broadcast_to(log_step.astype(F32)[:, None], (SSM_GROUPS, flat)),
      b_re.astype(F32).reshape(SSM_GROUPS, flat), b_im.astype(F32).reshape(SSM_GROUPS, flat))
    ab_re = ab_re[:, ::SSM_GROUP_CH]
    ab_im = ab_im[:, ::SSM_GROUP_CH]
    bb_re = bb_re.reshape(SSM_GROUPS, SSM_STATE, SSM_GROUP_CH)
    bb_im = bb_im.reshape(SSM_GROUPS, SSM_STATE, SSM_GROUP_CH)
    gh = SSM_GROUPS // 2
    eye = jnp.eye(gh, dtype=F32)

    def b_blockdiag(bb):
        t = bb.reshape(2, gh, SSM_STATE, SSM_GROUP_CH)
        t = jnp.einsum('fgph,gk->fghkp', t, eye)
        return t.reshape(2, gh * SSM_GROUP_CH, gh * SSM_STATE)

    def c_blockdiag(c):
        t = c.astype(F32).reshape(2, gh, SSM_GROUP_CH, SSM_STATE)
        t = jnp.einsum('fghp,gk->fgpkh', t, eye)
        return t.reshape(2, gh * SSM_STATE, gh * SSM_GROUP_CH)

    bbd = jnp.concatenate([b_blockdiag(bb_re), b_blockdiag(bb_im)], axis=-1).astype(BF16)
    cre_bd = c_blockdiag(c_re).astype(BF16)
    ncim_bd = (-c_blockdiag(c_im)).astype(BF16)
    are8 = jnp.broadcast_to(ab_re.reshape(1, SSM_CH), (nb, SSM_CH))
    aim8 = jnp.broadcast_to(ab_im.reshape(1, SSM_CH), (nb, SSM_CH))
    return are8, aim8, bbd, cre_bd, ncim_bd


def _sgu_mix(u_ref, v_ref, gt_ref, lg_ref, lb_ref, w_ref, bt_ref, o_ref, tm):
    v = v_ref[...].astype(F32)
    mu = jnp.mean(v, axis=-1, keepdims=True)
    vc = v - mu
    var = jnp.mean(vc * vc, axis=-1, keepdims=True)
    vn = (vc * lax.rsqrt(var + LN_EPS) * lg_ref[...] + lb_ref[...]).astype(BF16)
    row = lax.broadcasted_iota(jnp.int32, (CHUNK, CHUNK), 0)
    col = lax.broadcasted_iota(jnp.int32, (CHUNK, CHUNK), 1)
    for hd in range(SGU_HEADS):
        w = jnp.where(col <= row, w_ref[hd], 0.0).astype(BF16)
        bias = bt_ref[:, hd:hd + 1]
        cs = slice(hd * SGU_HEAD_W, (hd + 1) * SGU_HEAD_W)
        for c in range(tm // CHUNK):
            rs = slice(c * CHUNK, (c + 1) * CHUNK)
            s = jnp.dot(w, vn[rs, cs], preferred_element_type=F32) + bias
            uu = u_ref[rs, cs].astype(F32)
            gg = gt_ref[rs, cs].astype(F32)
            o_ref[rs, cs] = (uu * s * _silu(gg)).astype(o_ref.dtype)


def _attn_kernel(q_ref, k_ref, v_ref, gt_ref, lq1_ref, lk1_ref, lq2_ref, lk2_ref, sg_ref,
                 o_ref, qzt_ref, vt_ref, s_ref, m_ref, acc_ref, fin_ref, *, tq, tk, lam_init):
    nkb = vt_ref.shape[0]
    nq = nkb // 2

    pad = lax.broadcasted_iota(jnp.int32, (ATT_PAD_ROWS, tk), 0)
    ones_row = jnp.where(pad == 0, 1.0, 0.0).astype(BF16)
    for j in range(nkb):
        vt_ref[j, :ATT_HEAD_W, :] = v_ref[j * tk:(j + 1) * tk, :].T
        vt_ref[j, ATT_HEAD_W:, :] = ones_row

    lam = (jnp.exp(jnp.sum(lq1_ref[...] * lk1_ref[...], axis=-1, keepdims=True))
           - jnp.exp(jnp.sum(lq2_ref[...] * lk2_ref[...], axis=-1, keepdims=True)) + lam_init)

    def q_rows(qi):
        return pl.ds(pl.multiple_of(qi * tq, tq), tq)

    def k_rows(kb):
        return pl.ds(pl.multiple_of(kb * tk, tk), tk)

    def prep_q(qi):
        qt = q_ref[q_rows(qi), :].T
        sub = lax.broadcasted_iota(jnp.int32, (ATT_HEAD_W, tq), 0)
        qzt_ref[0] = jnp.where(sub < ATT_HEAD_D, qt, 0).astype(BF16)
        qzt_ref[1] = jnp.where(sub >= ATT_HEAD_D, qt, 0).astype(BF16)

    def scores(st, kb, lo=0):
        s_ref[st, :, lo:] = jnp.dot(k_ref[k_rows(kb), :], qzt_ref[st, :, lo:],
                                    preferred_element_type=F32)

    def softmax_pv(st, kb, masked, lo=0):
        s = s_ref[st, :, lo:]
        if masked:
            key = lax.broadcasted_iota(jnp.int32, (tk, tq - lo), 0)
            qry = lax.broadcasted_iota(jnp.int32, (tk, tq - lo), 1)
            s = jnp.where(key <= qry, s, MASK_VALUE)
        m = m_ref[st, :, lo:]
        m_new = jnp.maximum(m, jnp.max(s, axis=0, keepdims=True))
        alpha = jnp.exp2(m - m_new)
        p = jnp.exp2(s - m_new).astype(BF16)
        acc_ref[st, :, lo:] = (alpha * acc_ref[st, :, lo:]
                               + jnp.dot(vt_ref[kb], p, preferred_element_type=F32))
        m_ref[st, :, lo:] = m_new

    def reset_stats():
        m_ref[...] = jnp.full(m_ref.shape, MASK_VALUE, F32)
        acc_ref[...] = jnp.zeros(acc_ref.shape, F32)

    def finish(qi):
        a1 = fin_ref[0]
        a2 = fin_ref[1]
        o1 = a1[:ATT_HEAD_W] * (1.0 / a1[ATT_HEAD_W:ATT_HEAD_W + 1])
        o2 = a2[:ATT_HEAD_W] * (1.0 / a2[ATT_HEAD_W:ATT_HEAD_W + 1])
        ot = o1 - lam * o2
        o = _rms(ot.T, sg_ref[...]) * (1.0 - lam_init)
        rows = q_rows(qi)
        o_ref[rows, :] = (o * _silu(gt_ref[rows, :].astype(F32))).astype(o_ref.dtype)

    prep_q(0)
    reset_stats()
    den = lax.broadcasted_iota(jnp.int32, fin_ref.shape, 1)
    fin_ref[...] = jnp.where(den == ATT_HEAD_W, 1.0, 0.0)
    scores(0, 0)

    def q_block(qi, c):
        def k_block(kb):
            scores(1, kb)
            softmax_pv(0, kb, False)
            scores(0, kb + 1)
            softmax_pv(1, kb, False)

        def k_pair(j, c2):
            k_block(2 * j)
            k_block(2 * j + 1)
            return c2

        lax.fori_loop(0, qi, k_pair, 0)
        da = 2 * qi
        scores(1, da)
        softmax_pv(0, da, True)
        scores(0, da + 1, tk)
        softmax_pv(1, da, True)
        scores(1, da + 1, tk)
        softmax_pv(0, da + 1, True, tk)
        prep_q(jnp.minimum(qi + 1, nq - 1))
        scores(0, 0)
        finish(jnp.maximum(qi - 1, 0))
        softmax_pv(1, da + 1, True, tk)
        fin_ref[...] = acc_ref[...]
        reset_stats()
        return c

    lax.fori_loop(0, nq, q_block, 0)
    finish(nq - 1)


def _attention(z_rest, lq1, lk1, lq2, lk2, subln_g, *, nbatch, seq, tk, lam_init):
    tq = 2 * tk
    const = lambda *shape: pl.BlockSpec(shape, lambda b, h: (0,) * len(shape))
    return pl.pallas_call(
        functools.partial(_attn_kernel, tq=tq, tk=tk, lam_init=lam_init),
        grid=(nbatch, ATT_HEADS),
        in_specs=[
            pl.BlockSpec((seq, ATT_HEAD_W), lambda b, h: (b, ATT_Q_OFF + h)),
            pl.BlockSpec((seq, ATT_HEAD_W), lambda b, h: (b, ATT_K_OFF + h)),
            pl.BlockSpec((seq, ATT_HEAD_W), lambda b, h: (b, ATT_V_OFF + h)),
            pl.BlockSpec((seq, ATT_HEAD_W), lambda b, h: (b, ATT_G_OFF + h)),
            const(1, ATT_HEAD_D),
            const(1, ATT_HEAD_D),
            const(1, ATT_HEAD_D),
            const(1, ATT_HEAD_D),
            const(1, ATT_HEAD_W),
        ],
        out_specs=pl.BlockSpec((seq, ATT_HEAD_W), lambda b, h: (b, h)),
        out_shape=jax.ShapeDtypeStruct((nbatch * seq, ATT_W), BF16),
        scratch_shapes=[pltpu.VMEM((2, ATT_HEAD_W, tq), BF16),
                        pltpu.VMEM((seq // tk, ATT_HEAD_W + ATT_PAD_ROWS, tk), BF16),
                        pltpu.VMEM((2, tk, tq), F32),
                        pltpu.VMEM((2, 1, tq), F32),
                        pltpu.VMEM((2, ATT_HEAD_W + ATT_PAD_ROWS, tq), F32),
                        pltpu.VMEM((2, ATT_HEAD_W + ATT_PAD_ROWS, tq), F32)],
        compiler_params=pltpu.CompilerParams(
            dimension_semantics=("parallel", "parallel"),
            vmem_limit_bytes=V7X_VMEM_LIMIT_BYTES),
        name="diff_attention",
    )(z_rest, z_rest, z_rest, z_rest, lq1, lk1, lq2, lk2, subln_g)


def _outproj_kernel(ys_ref, u_ref, v_ref, gt_ref, lg_ref, lb_ref, ws_ref, bt_ref, ya_ref, w_ref,
                    x_ref, fg_ref, o_ref, yg_ref, *, tm, final_norm):
    _sgu_mix(u_ref, v_ref, gt_ref, lg_ref, lb_ref, ws_ref, bt_ref, yg_ref, tm)
    acc = jnp.dot(ys_ref[...], w_ref[:SSM_W, :], preferred_element_type=F32)
    acc = acc + jnp.dot(yg_ref[...], w_ref[SSM_W:SSM_W + SGU_W, :], preferred_element_type=F32)
    acc = acc + jnp.dot(ya_ref[...], w_ref[SSM_W + SGU_W:, :], preferred_element_type=F32)
    xn = x_ref[...] + acc
    if final_norm:
        xn = _rms(xn, fg_ref[...])
    o_ref[...] = xn


def _outproj(y_ssm, z_rest, sgu_ln_g, sgu_ln_b, sgu_w, sgu_bt, y_att, w_out, x2d, final_g, *,
             tm, final_norm):
    m, d = x2d.shape
    assert tm % CHUNK == 0
    return pl.pallas_call(
        functools.partial(_outproj_kernel, tm=tm, final_norm=final_norm),
        grid=(m // tm,),
        in_specs=[
            pl.BlockSpec((tm, SSM_W), lambda i: (i, 0)),
            pl.BlockSpec((tm, SGU_W), lambda i: (i, SGU_U_BLK)),
            pl.BlockSpec((tm, SGU_W), lambda i: (i, SGU_V_BLK)),
            pl.BlockSpec((tm, SGU_W), lambda i: (i, SGU_G_BLK)),
            _resident(1, SGU_W),
            _resident(1, SGU_W),
            _resident(SGU_HEADS, CHUNK, CHUNK),
            _resident(CHUNK, SGU_HEADS),
            pl.BlockSpec((tm, ATT_W), lambda i: (i, 0)),
            _resident(D_MIX, d),
            pl.BlockSpec((tm, d), lambda i: (i, 0)),
            _resident(1, d),
        ],
        out_specs=pl.BlockSpec((tm, d), lambda i: (i, 0)),
        out_shape=jax.ShapeDtypeStruct((m, d), F32),
        scratch_shapes=[pltpu.VMEM((tm, SGU_W), BF16)],
        compiler_params=pltpu.CompilerParams(
            dimension_semantics=("parallel",),
            vmem_limit_bytes=V7X_VMEM_LIMIT_BYTES),
        name="outproj",
    )(y_ssm, z_rest, z_rest, z_rest, sgu_ln_g, sgu_ln_b, sgu_w, sgu_bt, y_att, w_out, x2d, final_g)


def _pick(n, prefs):
    for p in prefs:
        if n % p == 0:
            return p
    raise ValueError(f"no supported tile for extent {n}")


def kernel(x, norm_g, w_in, ssm_a_re, ssm_a_im, ssm_log_step, ssm_b_re, ssm_b_im, ssm_c_re, ssm_c_im,
           ssm_d, glu_w, glu_b, sgu_ln_g, sgu_ln_b, sgu_w, sgu_b, lam_q1, lam_k1, lam_q2, lam_k2,
           attn_subln_g, w_out, final_g):
    nbatch, seq, d = x.shape
    depth = norm_g.shape[0]
    assert d == D_MODEL and seq % 1024 == 0 and nbatch == 8
    m = nbatch * seq
    tm_in = _pick(m, (1024, 512, 256))
    tm_out = _pick(m, (1024, 512, 256))
    tt = _pick(seq, (64, 32))
    tk_att = 512
    q_lo = 3 * SGU_W
    rest_scale = jnp.ones((REST_COLS,), F32).at[q_lo:q_lo + ATT_W].set(ATT_HEAD_D ** -0.5 * LOG2_E)

    x2d = x.reshape(m, d).astype(F32)
    fg = final_g.reshape(1, d).astype(F32)
    for l in range(depth):
        g = norm_g[l].reshape(1, d).astype(F32)
        w_rest = (w_in[l, :, SSM_COLS:].astype(F32) * rest_scale).astype(BF16)
        w_ssm = w_in[l, :, :SSM_COLS].astype(BF16)
        lam_init = 0.8 - 0.6 * math.exp(-0.3 * l)

        z_rest = _inproj(x2d, g, w_rest, tm=tm_in)

        ssm_p = _ssm_params(ssm_a_re[l], ssm_a_im[l], ssm_log_step[l], ssm_b_re[l], ssm_b_im[l],
                            ssm_c_re[l], ssm_c_im[l], nbatch)
        y_ssm_tb = _ssm_branch(x2d.reshape(nbatch, seq, d), g, w_ssm, *ssm_p,
                               ssm_d[l].reshape(1, SSM_W).astype(F32), glu_w[l].astype(BF16),
                               glu_b[l].reshape(1, SSM_W).astype(F32), tt=tt)
        y_ssm = y_ssm_tb.reshape(seq, nbatch, SSM_W).swapaxes(0, 1).reshape(m, SSM_W)

        y_att = _attention(z_rest, lam_q1[l].reshape(1, -1).astype(F32), lam_k1[l].reshape(1, -1).astype(F32),
                           lam_q2[l].reshape(1, -1).astype(F32), lam_k2[l].reshape(1, -1).astype(F32),
                           attn_subln_g[l].reshape(1, ATT_HEAD_W).astype(F32),
                           nbatch=nbatch, seq=seq, tk=tk_att, lam_init=lam_init)

        x2d = _outproj(y_ssm, z_rest, sgu_ln_g[l].reshape(1, SGU_W).astype(F32),
                       sgu_ln_b[l].reshape(1, SGU_W).astype(F32), sgu_w[l].astype(F32),
                       sgu_b[l].astype(F32).T, y_att, w_out[l].astype(BF16), x2d, fg,
                       tm=tm_out, final_norm=(l == depth - 1))
    return x2d.reshape(nbatch, seq, d).astype(x.dtype)
```

```python
import functools
import math

import jax
import jax.numpy as jnp
from jax import lax
from jax.experimental import pallas as pl
from jax.experimental.pallas import tpu as pltpu

F32 = jnp.float32
BF16 = jnp.bfloat16

D_MODEL = 1024
D_MIX = 2 * D_MODEL
SSM_W = D_MIX // 4
SSM_GROUP_CH = 16
SSM_GROUPS = SSM_W // SSM_GROUP_CH
SSM_STATE = 64
SSM_CH = SSM_GROUPS * SSM_STATE
SSM_HALF_IN = SSM_W // 2
SSM_HALF_CH = SSM_CH // 2
SGU_W = D_MIX // 4
CHUNK = 128
SGU_HEADS = 4
SGU_HEAD_W = SGU_W // SGU_HEADS
ATT_W = D_MIX // 2
ATT_HEAD_D = 64
ATT_HEADS = ATT_W // (2 * ATT_HEAD_D)
ATT_HEAD_W = 2 * ATT_HEAD_D
ATT_PAD_ROWS = 16
NORM_EPS = 1e-6
LN_EPS = 1e-5

SSM_COLS = 2 * SSM_W
REST_COLS = 3 * SGU_W + 4 * ATT_W
SGU_U_BLK, SGU_V_BLK, SGU_G_BLK = 0, 1, 2
ATT_Q_OFF = 3 * SGU_W // ATT_HEAD_W
ATT_K_OFF = ATT_Q_OFF + ATT_HEADS
ATT_V_OFF = ATT_K_OFF + ATT_HEADS
ATT_G_OFF = ATT_V_OFF + ATT_HEADS

V7X_VMEM_LIMIT_BYTES = 56 * 1024 * 1024
MXU_COL_CHUNK = 1024
LOG2_E = math.log2(math.e)
MASK_VALUE = -1e30


def _rms(x, g):
    ms = jnp.mean(x * x, axis=-1, keepdims=True)
    return x * lax.rsqrt(ms + NORM_EPS) * g


def _silu(x):
    return x * (1.0 / (1.0 + jnp.exp(-x)))


def _sigmoid(x):
    return 1.0 / (1.0 + jnp.exp(-x))


def _gelu_exact(x):
    return 0.5 * x * (1.0 + lax.erf(x * math.sqrt(0.5)))


def _inproj_kernel(x_ref, g_ref, w_ref, o_ref, *, col_chunks):
    h = _rms(x_ref[...], g_ref[...]).astype(BF16)
    c0 = 0
    for cw in col_chunks:
        o_ref[:, c0:c0 + cw] = jnp.dot(h, w_ref[:, c0:c0 + cw],
                                       preferred_element_type=F32).astype(o_ref.dtype)
        c0 += cw


def _resident(*shape):
    return pl.BlockSpec(shape, lambda *_: (0,) * len(shape), pipeline_mode=pl.Buffered(1))


def _layer(l, *shape):
    return pl.BlockSpec((None,) + shape, lambda *_: (l,) + (0,) * len(shape),
                        pipeline_mode=pl.Buffered(1))


def _inproj(x2d, g, w, *, layer, tm):
    m, d = x2d.shape
    n = w.shape[-1]
    col_chunks = (MXU_COL_CHUNK,) * (n // MXU_COL_CHUNK)
    if n % MXU_COL_CHUNK:
        col_chunks += (n % MXU_COL_CHUNK,)
    return pl.pallas_call(
        functools.partial(_inproj_kernel, col_chunks=col_chunks),
        grid=(m // tm,),
        in_specs=[
            pl.BlockSpec((tm, d), lambda i: (i, 0)),
            _layer(layer, 1, d),
            _layer(layer, d, n),
        ],
        out_specs=pl.BlockSpec((tm, n), lambda i: (i, 0)),
        out_shape=jax.ShapeDtypeStruct((m, n), BF16),
        compiler_params=pltpu.CompilerParams(
            dimension_semantics=("parallel",),
            vmem_limit_bytes=V7X_VMEM_LIMIT_BYTES),
        name="inproj",
    )(x2d, g, w)


def _ssm_kernel(x_hbm, g_ref, w_ref, are_ref, aim_ref, bbd_ref, cre_ref, ncim_ref, d_ref,
                gw_ref, gb_ref, o_ref, xbuf, sem, z_ref, bre_ref, bim_ref, carry_ref,
                *, tt, nb, nblk):
    j = pl.program_id(0)
    r = tt * nb

    def x_copies(slot, blk):
        return [pltpu.make_async_copy(x_hbm.at[b, pl.ds(blk * tt, tt), :], xbuf.at[slot, :, b, :],
                                      sem.at[slot, b]) for b in range(nb)]

    def fetch(slot, blk):
        for c in x_copies(slot, blk):
            c.start()

    def await_x(slot, blk):
        for c in x_copies(slot, blk):
            c.wait()

    def project(slot):
        x = xbuf[slot].reshape(r, x_hbm.shape[-1])
        h = _rms(x, g_ref[...]).astype(BF16)
        z = jnp.dot(h, w_ref[...], preferred_element_type=F32)
        z_ref[slot] = z
        ub = z[:, :SSM_W].astype(BF16)
        for hf in range(2):
            bu = jnp.dot(ub[:, hf * SSM_HALF_IN:(hf + 1) * SSM_HALF_IN], bbd_ref[hf],
                         preferred_element_type=F32)
            bre_ref[slot, :, hf * SSM_HALF_CH:(hf + 1) * SSM_HALF_CH] = bu[:, :SSM_HALF_CH]
            bim_ref[slot, :, hf * SSM_HALF_CH:(hf + 1) * SSM_HALF_CH] = bu[:, SSM_HALF_CH:]

    def scan(slot):
        sr = carry_ref[0]
        si = carry_ref[1]
        ar = are_ref[...]
        ai = aim_ref[...]
        for t in range(tt):
            rows = slice(t * nb, (t + 1) * nb)
            nsr = ar * sr - ai * si + bre_ref[slot, rows, :]
            nsi = ar * si + ai * sr + bim_ref[slot, rows, :]
            bre_ref[slot, rows, :] = nsr
            bim_ref[slot, rows, :] = nsi
            sr, si = nsr, nsi
        carry_ref[0] = sr
        carry_ref[1] = si

    def emit(slot):
        u = z_ref[slot, :, :SSM_W]
        gate = z_ref[slot, :, SSM_W:]
        ys = []
        for hf in range(2):
            cols = slice(hf * SSM_HALF_CH, (hf + 1) * SSM_HALF_CH)
            yh = jnp.dot(bre_ref[slot, :, cols].astype(BF16), cre_ref[hf], preferred_element_type=F32)
            yh = yh + jnp.dot(bim_ref[slot, :, cols].astype(BF16), ncim_ref[hf],
                              preferred_element_type=F32)
            ys.append(yh)
        y = jnp.concatenate(ys, axis=-1) + d_ref[...] * u
        y = _gelu_exact(y)
        y = y * _sigmoid(jnp.dot(y.astype(BF16), gw_ref[...], preferred_element_type=F32) + gb_ref[...])
        o_ref[slot * r:(slot + 1) * r, :] = (y * _silu(gate)).astype(o_ref.dtype)

    @pl.when(j == 0)
    def _():
        carry_ref[...] = jnp.zeros_like(carry_ref)
        fetch(0, 0)
        await_x(0, 0)
        project(0)
        fetch(1, 1)

    has_next = 2 * j + 2 < nblk

    @pl.when(has_next)
    def _():
        fetch(0, 2 * j + 2)

    await_x(1, 2 * j + 1)
    project(1)
    scan(0)
    emit(0)

    @pl.when(has_next)
    def _():
        fetch(1, 2 * j + 3)
        await_x(0, 2 * j + 2)

    project(0)
    scan(1)
    emit(1)


def _ssm_branch(x3d, g, w_ssm, are8, aim8, bbd, cre_bd, ncim_bd, d, glu_w, glu_b, *, layer, tt):
    nb, seq, dm = x3d.shape
    r = tt * nb
    nblk = seq // tt
    assert nblk % 2 == 0
    return pl.pallas_call(
        functools.partial(_ssm_kernel, tt=tt, nb=nb, nblk=nblk),
        grid=(nblk // 2,),
        in_specs=[
            pl.BlockSpec(memory_space=pl.ANY),
            _layer(layer, 1, dm),
            _layer(layer, dm, SSM_COLS),
            _layer(layer, nb, SSM_CH),
            _layer(layer, nb, SSM_CH),
            _layer(layer, 2, SSM_HALF_IN, 2 * SSM_HALF_CH),
            _layer(layer, 2, SSM_HALF_CH, SSM_HALF_IN),
            _layer(layer, 2, SSM_HALF_CH, SSM_HALF_IN),
            _layer(layer, 1, SSM_W),
            _layer(layer, SSM_W, SSM_W),
            _layer(layer, 1, SSM_W),
        ],
        out_specs=pl.BlockSpec((2 * r, SSM_W), lambda i: (i, 0)),
        out_shape=jax.ShapeDtypeStruct((seq * nb, SSM_W), BF16),
        scratch_shapes=[
            pltpu.VMEM((2, tt, nb, dm), F32),
            pltpu.SemaphoreType.DMA((2, nb)),
            pltpu.VMEM((2, r, SSM_COLS), F32),
            pltpu.VMEM((2, r, SSM_CH), F32),
            pltpu.VMEM((2, r, SSM_CH), F32),
            pltpu.VMEM((2, nb, SSM_CH), F32),
        ],
        compiler_params=pltpu.CompilerParams(
            dimension_semantics=("arbitrary",),
            vmem_limit_bytes=V7X_VMEM_LIMIT_BYTES),
        name="ssm_branch",
    )(x3d, g, w_ssm, are8, aim8, bbd, cre_bd, ncim_bd, d, glu_w, glu_b)


def _ssm_zoh_kernel(lr_ref, li_ref, ls_ref, br_ref, bi_ref, are_ref, aim_ref, bbr_ref, bbi_ref):
    lr = lr_ref[...]
    li = li_ref[...]
    step = jnp.exp(ls_ref[...])
    mag = jnp.exp(step * lr)
    ang = step * li
    ab_re = mag * jnp.cos(ang)
    ab_im = mag * jnp.sin(ang)
    den = lr * lr + li * li
    nr = ab_re - 1.0
    ni = ab_im
    co_re = (nr * lr + ni * li) / den
    co_im = (ni * lr - nr * li) / den
    br = br_ref[...]
    bi = bi_ref[...]
    are_ref[...] = ab_re
    aim_ref[...] = ab_im
    bbr_ref[...] = co_re * br - co_im * bi
    bbi_ref[...] = co_re * bi + co_im * br


def _ssm_params(a_re, a_im, log_step, b_re, b_im, c_re, c_im, nb):
    depth = a_re.shape[0]
    rows = depth * SSM_GROUPS
    flat = SSM_STATE * SSM_GROUP_CH
    rep = lambda a: jnp.repeat(a.astype(F32).reshape(rows, SSM_STATE), SSM_GROUP_CH, axis=-1)
    out = jax.ShapeDtypeStruct((rows, flat), F32)
    ab_re, ab_im, bb_re, bb_im = pl.pallas_call(
        _ssm_zoh_kernel, out_shape=(out, out, out, out), name="ssm_zoh",
    )(rep(a_re), rep(a_im),
      jnp.broadcast_to(log_step.astype(F32).reshape(rows, 1), (rows, flat)),
      b_re.astype(F32).reshape(rows, flat), b_im.astype(F32).reshape(rows, flat))
    gh = SSM_GROUPS // 2
    eye = jnp.eye(gh, dtype=F32)

    def b_blockdiag(bb):
        t = bb.reshape(depth, 2, gh, SSM_STATE, SSM_GROUP_CH)
        t = jnp.einsum('dfgph,gk->dfghkp', t, eye)
        return t.reshape(depth, 2, gh * SSM_GROUP_CH, gh * SSM_STATE)

    def c_blockdiag(c):
        t = c.astype(F32).reshape(depth, 2, gh, SSM_GROUP_CH, SSM_STATE)
        t = jnp.einsum('dfghp,gk->dfgpkh', t, eye)
        return t.reshape(depth, 2, gh * SSM_STATE, gh * SSM_GROUP_CH)

    def a_rows(ab):
        ab = ab[:, ::SSM_GROUP_CH].reshape(depth, 1, SSM_CH)
        return jnp.broadcast_to(ab, (depth, nb, SSM_CH))

    bbd = jnp.concatenate([b_blockdiag(bb_re), b_blockdiag(bb_im)], axis=-1).astype(BF16)
    cre_bd = c_blockdiag(c_re).astype(BF16)
    ncim_bd = (-c_blockdiag(c_im)).astype(BF16)
    return a_rows(ab_re), a_rows(ab_im), bbd, cre_bd, ncim_bd


def _sgu_mix(u_ref, v_ref, gt_ref, lg_ref, lb_ref, w_ref, bt_ref, o_ref, tm):
    v = v_ref[...].astype(F32)
    mu = jnp.mean(v, axis=-1, keepdims=True)
    vc = v - mu
    var = jnp.mean(vc * vc, axis=-1, keepdims=True)
    vn = (vc * lax.rsqrt(var + LN_EPS) * lg_ref[...] + lb_ref[...]).astype(BF16)
    row = lax.broadcasted_iota(jnp.int32, (CHUNK, CHUNK), 0)
    col = lax.broadcasted_iota(jnp.int32, (CHUNK, CHUNK), 1)
    for hd in range(SGU_HEADS):
        w = jnp.where(col <= row, w_ref[hd], 0.0).astype(BF16)
        bias = bt_ref[:, hd:hd + 1]
        cs = slice(hd * SGU_HEAD_W, (hd + 1) * SGU_HEAD_W)
        for c in range(tm // CHUNK):
            rs = slice(c * CHUNK, (c + 1) * CHUNK)
            s = jnp.dot(w, vn[rs, cs], preferred_element_type=F32) + bias
            uu = u_ref[rs, cs].astype(F32)
            gg = gt_ref[rs, cs].astype(F32)
            o_ref[rs, cs] = (uu * s * _silu(gg)).astype(o_ref.dtype)


def _attn_kernel(q_ref, k_ref, v_ref, gt_ref, lq1_ref, lk1_ref, lq2_ref, lk2_ref, sg_ref,
                 o_ref, qzt_ref, vt_ref, s_ref, m_ref, acc_ref, fin_ref, *, tq, tk, lam_init):
    nkb = vt_ref.shape[0]
    nq = nkb // 2

    pad = lax.broadcasted_iota(jnp.int32, (ATT_PAD_ROWS, tk), 0)
    ones_row = jnp.where(pad == 0, 1.0, 0.0).astype(BF16)
    for j in range(nkb):
        vt_ref[j, :ATT_HEAD_W, :] = v_ref[j * tk:(j + 1) * tk, :].T
        vt_ref[j, ATT_HEAD_W:, :] = ones_row

    lam = (jnp.exp(jnp.sum(lq1_ref[...] * lk1_ref[...], axis=-1, keepdims=True))
           - jnp.exp(jnp.sum(lq2_ref[...] * lk2_ref[...], axis=-1, keepdims=True)) + lam_init)

    def q_rows(qi):
        return pl.ds(pl.multiple_of(qi * tq, tq), tq)

    def k_rows(kb):
        return pl.ds(pl.multiple_of(kb * tk, tk), tk)

    def prep_q(qi):
        qt = q_ref[q_rows(qi), :].T
        sub = lax.broadcasted_iota(jnp.int32, (ATT_HEAD_W, tq), 0)
        qzt_ref[0] = jnp.where(sub < ATT_HEAD_D, qt, 0).astype(BF16)
        qzt_ref[1] = jnp.where(sub >= ATT_HEAD_D, qt, 0).astype(BF16)

    def scores(st, kb, lo=0):
        s_ref[st, :, lo:] = jnp.dot(k_ref[k_rows(kb), :], qzt_ref[st, :, lo:],
                                    preferred_element_type=F32)

    def softmax_pv(st, kb, masked, lo=0):
        s = s_ref[st, :, lo:]
        if masked:
            key = lax.broadcasted_iota(jnp.int32, (tk, tq - lo), 0)
            qry = lax.broadcasted_iota(jnp.int32, (tk, tq - lo), 1)
            s = jnp.where(key <= qry, s, MASK_VALUE)
        m = m_ref[st, :, lo:]
        m_new = jnp.maximum(m, jnp.max(s, axis=0, keepdims=True))
        alpha = jnp.exp2(m - m_new)
        p = jnp.exp2(s - m_new).astype(BF16)
        acc_ref[st, :, lo:] = (alpha * acc_ref[st, :, lo:]
                               + jnp.dot(vt_ref[kb], p, preferred_element_type=F32))
        m_ref[st, :, lo:] = m_new

    def reset_stats():
        m_ref[...] = jnp.full(m_ref.shape, MASK_VALUE, F32)
        acc_ref[...] = jnp.zeros(acc_ref.shape, F32)

    def finish(qi):
        a1 = fin_ref[0]
        a2 = fin_ref[1]
        o1 = a1[:ATT_HEAD_W] * (1.0 / a1[ATT_HEAD_W:ATT_HEAD_W + 1])
        o2 = a2[:ATT_HEAD_W] * (1.0 / a2[ATT_HEAD_W:ATT_HEAD_W + 1])
        ot = o1 - lam * o2
        o = _rms(ot.T, sg_ref[...]) * (1.0 - lam_init)
        rows = q_rows(qi)
        o_ref[rows, :] = (o * _silu(gt_ref[rows, :].astype(F32))).astype(o_ref.dtype)

    prep_q(0)
    reset_stats()
    den = lax.broadcasted_iota(jnp.int32, fin_ref.shape, 1)
    fin_ref[...] = jnp.where(den == ATT_HEAD_W, 1.0, 0.0)
    scores(0, 0)

    def q_block(qi, c):
        def k_block(kb):
            scores(1, kb)
            softmax_pv(0, kb, False)
            scores(0, kb + 1)
            softmax_pv(1, kb, False)

        def k_pair(j, c2):
            k_block(2 * j)
            k_block(2 * j + 1)
            return c2

        lax.fori_loop(0, qi, k_pair, 0)
        da = 2 * qi
        scores(1, da)
        softmax_pv(0, da, True)
        scores(0, da + 1, tk)
        softmax_pv(1, da, True)
        scores(1, da + 1, tk)
        softmax_pv(0, da + 1, True, tk)
        prep_q(jnp.minimum(qi + 1, nq - 1))
        scores(0, 0)
        finish(jnp.maximum(qi - 1, 0))
        softmax_pv(1, da + 1, True, tk)
        fin_ref[...] = acc_ref[...]
        reset_stats()
        return c

    lax.fori_loop(0, nq, q_block, 0)
    finish(nq - 1)


def _attention(z_rest, lq1, lk1, lq2, lk2, subln_g, *, layer, nbatch, seq, tk, lam_init):
    tq = 2 * tk
    const = functools.partial(_layer, layer)
    return pl.pallas_call(
        functools.partial(_attn_kernel, tq=tq, tk=tk, lam_init=lam_init),
        grid=(nbatch, ATT_HEADS),
        in_specs=[
            pl.BlockSpec((seq, ATT_HEAD_W), lambda b, h: (b, ATT_Q_OFF + h)),
            pl.BlockSpec((seq, ATT_HEAD_W), lambda b, h: (b, ATT_K_OFF + h)),
            pl.BlockSpec((seq, ATT_HEAD_W), lambda b, h: (b, ATT_V_OFF + h)),
            pl.BlockSpec((seq, ATT_HEAD_W), lambda b, h: (b, ATT_G_OFF + h)),
            const(1, ATT_HEAD_D),
            const(1, ATT_HEAD_D),
            const(1, ATT_HEAD_D),
            const(1, ATT_HEAD_D),
            const(1, ATT_HEAD_W),
        ],
        out_specs=pl.BlockSpec((seq, ATT_HEAD_W), lambda b, h: (b, h)),
        out_shape=jax.ShapeDtypeStruct((nbatch * seq, ATT_W), BF16),
        scratch_shapes=[pltpu.VMEM((2, ATT_HEAD_W, tq), BF16),
                        pltpu.VMEM((seq // tk, ATT_HEAD_W + ATT_PAD_ROWS, tk), BF16),
                        pltpu.VMEM((2, tk, tq), F32),
                        pltpu.VMEM((2, 1, tq), F32),
                        pltpu.VMEM((2, ATT_HEAD_W + ATT_PAD_ROWS, tq), F32),
                        pltpu.VMEM((2, ATT_HEAD_W + ATT_PAD_ROWS, tq), F32)],
        compiler_params=pltpu.CompilerParams(
            dimension_semantics=("parallel", "parallel"),
            vmem_limit_bytes=V7X_VMEM_LIMIT_BYTES),
        name="diff_attention",
    )(z_rest, z_rest, z_rest, z_rest, lq1, lk1, lq2, lk2, subln_g)


def _outproj_kernel(ys_ref, u_ref, v_ref, gt_ref, lg_ref, lb_ref, ws_ref, bt_ref, ya_ref, w_ref,
                    x_ref, fg_ref, o_ref, yg_ref, *, tm, final_norm):
    _sgu_mix(u_ref, v_ref, gt_ref, lg_ref, lb_ref, ws_ref, bt_ref, yg_ref, tm)
    acc = jnp.dot(ys_ref[...], w_ref[:SSM_W, :], preferred_element_type=F32)
    acc = acc + jnp.dot(yg_ref[...], w_ref[SSM_W:SSM_W + SGU_W, :], preferred_element_type=F32)
    acc = acc + jnp.dot(ya_ref[...], w_ref[SSM_W + SGU_W:, :], preferred_element_type=F32)
    xn = x_ref[...] + acc
    if final_norm:
        xn = _rms(xn, fg_ref[...])
    o_ref[...] = xn


def _outproj(y_ssm, z_rest, sgu_ln_g, sgu_ln_b, sgu_w, sgu_bt, y_att, w_out, x2d, final_g, *,
             layer, tm, final_norm):
    m, d = x2d.shape
    assert tm % CHUNK == 0
    return pl.pallas_call(
        functools.partial(_outproj_kernel, tm=tm, final_norm=final_norm),
        grid=(m // tm,),
        in_specs=[
            pl.BlockSpec((tm, SSM_W), lambda i: (i, 0)),
            pl.BlockSpec((tm, SGU_W), lambda i: (i, SGU_U_BLK)),
            pl.BlockSpec((tm, SGU_W), lambda i: (i, SGU_V_BLK)),
            pl.BlockSpec((tm, SGU_W), lambda i: (i, SGU_G_BLK)),
            _layer(layer, 1, SGU_W),
            _layer(layer, 1, SGU_W),
            _layer(layer, SGU_HEADS, CHUNK, CHUNK),
            _layer(layer, CHUNK, SGU_HEADS),
            pl.BlockSpec((tm, ATT_W), lambda i: (i, 0)),
            _layer(layer, D_MIX, d),
            pl.BlockSpec((tm, d), lambda i: (i, 0)),
            _resident(1, d),
        ],
        out_specs=pl.BlockSpec((tm, d), lambda i: (i, 0)),
        out_shape=jax.ShapeDtypeStruct((m, d), F32),
        scratch_shapes=[pltpu.VMEM((tm, SGU_W), BF16)],
        compiler_params=pltpu.CompilerParams(
            dimension_semantics=("parallel",),
            vmem_limit_bytes=V7X_VMEM_LIMIT_BYTES),
        name="outproj",
    )(y_ssm, z_rest, z_rest, z_rest, sgu_ln_g, sgu_ln_b, sgu_w, sgu_bt, y_att, w_out, x2d, final_g)


def _pick(n, prefs):
    for p in prefs:
        if n % p == 0:
            return p
    raise ValueError(f"no supported tile for extent {n}")


def kernel(x, norm_g, w_in, ssm_a_re, ssm_a_im, ssm_log_step, ssm_b_re, ssm_b_im, ssm_c_re, ssm_c_im,
           ssm_d, glu_w, glu_b, sgu_ln_g, sgu_ln_b, sgu_w, sgu_b, lam_q1, lam_k1, lam_q2, lam_k2,
           attn_subln_g, w_out, final_g):
    nbatch, seq, d = x.shape
    depth = norm_g.shape[0]
    assert d == D_MODEL and seq % 1024 == 0 and nbatch == 8
    m = nbatch * seq
    tm_in = _pick(m, (1024, 512, 256))
    tm_out = _pick(m, (1024, 512, 256))
    tt = _pick(seq, (64, 32))
    tk_att = 512
    q_lo = 3 * SGU_W
    rest_scale = jnp.ones((REST_COLS,), F32).at[q_lo:q_lo + ATT_W].set(ATT_HEAD_D ** -0.5 * LOG2_E)

    vec = lambda a: a.astype(F32).reshape(depth, 1, -1)
    g_all = vec(norm_g)
    w_rest = (w_in[:, :, SSM_COLS:].astype(F32) * rest_scale).astype(BF16)
    w_ssm = w_in[:, :, :SSM_COLS].astype(BF16)
    ssm_p = _ssm_params(ssm_a_re, ssm_a_im, ssm_log_step, ssm_b_re, ssm_b_im, ssm_c_re, ssm_c_im, nbatch)
    ssm_d_all, glu_b_all, glu_w_all = vec(ssm_d), vec(glu_b), glu_w.astype(BF16)
    ln_g_all, ln_b_all = vec(sgu_ln_g), vec(sgu_ln_b)
    sgu_w_all, sgu_bt_all = sgu_w.astype(F32), jnp.swapaxes(sgu_b.astype(F32), 1, 2)
    lam_all = [vec(a) for a in (lam_q1, lam_k1, lam_q2, lam_k2)]
    subln_all = vec(attn_subln_g)
    w_out_all = w_out.astype(BF16)

    x2d = x.reshape(m, d).astype(F32)
    fg = final_g.reshape(1, d).astype(F32)
    for l in range(depth):
        lam_init = 0.8 - 0.6 * math.exp(-0.3 * l)

        z_rest = _inproj(x2d, g_all, w_rest, layer=l, tm=tm_in)

        y_ssm_tb = _ssm_branch(x2d.reshape(nbatch, seq, d), g_all, w_ssm, *ssm_p, ssm_d_all, glu_w_all,
                               glu_b_all, layer=l, tt=tt)
        y_ssm = y_ssm_tb.reshape(seq, nbatch, SSM_W).swapaxes(0, 1).reshape(m, SSM_W)

        y_att = _attention(z_rest, *lam_all, subln_all, layer=l, nbatch=nbatch, seq=seq, tk=tk_att,
                           lam_init=lam_init)

        x2d = _outproj(y_ssm, z_rest, ln_g_all, ln_b_all, sgu_w_all, sgu_bt_all, y_att, w_out_all, x2d, fg,
                       layer=l, tm=tm_out, final_norm=(l == depth - 1))
    return x2d.reshape(nbatch, seq, d).astype(x.dtype)
```

```python
import functools
import math

import jax
import jax.numpy as jnp
from jax import lax
from jax.experimental import pallas as pl
from jax.experimental.pallas import tpu as pltpu

F32 = jnp.float32
BF16 = jnp.bfloat16

D_MODEL = 1024
D_MIX = 2 * D_MODEL
SSM_W = D_MIX // 4
SSM_GROUP_CH = 16
SSM_GROUPS = SSM_W // SSM_GROUP_CH
SSM_STATE = 64
SSM_CH = SSM_GROUPS * SSM_STATE
SSM_HALF_IN = SSM_W // 2
SSM_HALF_CH = SSM_CH // 2
SGU_W = D_MIX // 4
CHUNK = 128
SGU_HEADS = 4
SGU_HEAD_W = SGU_W // SGU_HEADS
ATT_W = D_MIX // 2
ATT_HEAD_D = 64
ATT_HEADS = ATT_W // (2 * ATT_HEAD_D)
ATT_HEAD_W = 2 * ATT_HEAD_D
ATT_PAD_ROWS = 16
NORM_EPS = 1e-6
LN_EPS = 1e-5

SSM_COLS = 2 * SSM_W
REST_COLS = 3 * SGU_W + 4 * ATT_W
SGU_U_BLK, SGU_V_BLK, SGU_G_BLK = 0, 1, 2
ATT_Q_OFF = 3 * SGU_W // ATT_HEAD_W
ATT_K_OFF = ATT_Q_OFF + ATT_HEADS
ATT_V_OFF = ATT_K_OFF + ATT_HEADS
ATT_G_OFF = ATT_V_OFF + ATT_HEADS

V7X_VMEM_LIMIT_BYTES = 56 * 1024 * 1024
MXU_COL_CHUNK = 1024
LOG2_E = math.log2(math.e)
MASK_VALUE = -1e30


def _rms(x, g):
    ms = jnp.mean(x * x, axis=-1, keepdims=True)
    return x * lax.rsqrt(ms + NORM_EPS) * g


def _sigmoid(x):
    return 0.5 * (1.0 + jnp.tanh(0.5 * x))


def _silu(x):
    return x * _sigmoid(x)


def _gelu_exact(x):
    return 0.5 * x * (1.0 + lax.erf(x * math.sqrt(0.5)))


def _inproj_kernel(x_ref, g_ref, w_ref, o_ref, *, col_chunks):
    h = _rms(x_ref[...], g_ref[...]).astype(BF16)
    c0 = 0
    for cw in col_chunks:
        o_ref[:, c0:c0 + cw] = jnp.dot(h, w_ref[:, c0:c0 + cw],
                                       preferred_element_type=F32).astype(o_ref.dtype)
        c0 += cw


def _resident(*shape):
    return pl.BlockSpec(shape, lambda *_: (0,) * len(shape), pipeline_mode=pl.Buffered(1))


def _inproj(x2d, g, w, *, tm):
    m, d = x2d.shape
    n = w.shape[1]
    col_chunks = (MXU_COL_CHUNK,) * (n // MXU_COL_CHUNK)
    if n % MXU_COL_CHUNK:
        col_chunks += (n % MXU_COL_CHUNK,)
    return pl.pallas_call(
        functools.partial(_inproj_kernel, col_chunks=col_chunks),
        grid=(m // tm,),
        in_specs=[
            pl.BlockSpec((tm, d), lambda i: (i, 0)),
            _resident(1, d),
            _resident(d, n),
        ],
        out_specs=pl.BlockSpec((tm, n), lambda i: (i, 0)),
        out_shape=jax.ShapeDtypeStruct((m, n), BF16),
        compiler_params=pltpu.CompilerParams(
            dimension_semantics=("parallel",),
            vmem_limit_bytes=V7X_VMEM_LIMIT_BYTES),
        name="inproj",
    )(x2d, g, w)


def _ssm_kernel(x_hbm, g_ref, w_ref, are_ref, aim_ref, bbd_ref, cre_ref, ncim_ref, d_ref,
                gw_ref, gb_ref, o_ref, xbuf, sem, z_ref, bre_ref, bim_ref, carry_ref,
                *, tt, nb, nblk):
    j = pl.program_id(0)
    r = tt * nb

    def x_copies(slot, blk):
        return [pltpu.make_async_copy(x_hbm.at[b, pl.ds(blk * tt, tt), :], xbuf.at[slot, :, b, :],
                                      sem.at[slot, b]) for b in range(nb)]

    def fetch(slot, blk):
        for c in x_copies(slot, blk):
            c.start()

    def await_x(slot, blk):
        for c in x_copies(slot, blk):
            c.wait()

    def project(slot):
        x = xbuf[slot].reshape(r, x_hbm.shape[-1])
        h = _rms(x, g_ref[...]).astype(BF16)
        z = jnp.dot(h, w_ref[...], preferred_element_type=F32)
        z_ref[slot] = z
        ub = z[:, :SSM_W].astype(BF16)
        for hf in range(2):
            bu = jnp.dot(ub[:, hf * SSM_HALF_IN:(hf + 1) * SSM_HALF_IN], bbd_ref[hf],
                         preferred_element_type=F32)
            bre_ref[slot, :, hf * SSM_HALF_CH:(hf + 1) * SSM_HALF_CH] = bu[:, :SSM_HALF_CH]
            bim_ref[slot, :, hf * SSM_HALF_CH:(hf + 1) * SSM_HALF_CH] = bu[:, SSM_HALF_CH:]

    def scan(slot):
        sr = carry_ref[0]
        si = carry_ref[1]
        ar = are_ref[...]
        ai = aim_ref[...]
        for t in range(tt):
            rows = slice(t * nb, (t + 1) * nb)
            nsr = ar * sr - ai * si + bre_ref[slot, rows, :]
            nsi = ar * si + ai * sr + bim_ref[slot, rows, :]
            bre_ref[slot, rows, :] = nsr
            bim_ref[slot, rows, :] = nsi
            sr, si = nsr, nsi
        carry_ref[0] = sr
        carry_ref[1] = si

    def emit(slot):
        u = z_ref[slot, :, :SSM_W]
        gate = z_ref[slot, :, SSM_W:]
        ys = []
        for hf in range(2):
            cols = slice(hf * SSM_HALF_CH, (hf + 1) * SSM_HALF_CH)
            yh = jnp.dot(bre_ref[slot, :, cols].astype(BF16), cre_ref[hf], preferred_element_type=F32)
            yh = yh + jnp.dot(bim_ref[slot, :, cols].astype(BF16), ncim_ref[hf],
                              preferred_element_type=F32)
            ys.append(yh)
        y = jnp.concatenate(ys, axis=-1) + d_ref[...] * u
        y = _gelu_exact(y)
        y = y * _sigmoid(jnp.dot(y.astype(BF16), gw_ref[...], preferred_element_type=F32) + gb_ref[...])
        o_ref[slot * r:(slot + 1) * r, :] = (y * _silu(gate)).astype(o_ref.dtype)

    @pl.when(j == 0)
    def _():
        carry_ref[...] = jnp.zeros_like(carry_ref)
        fetch(0, 0)
        await_x(0, 0)
        project(0)
        fetch(1, 1)

    has_next = 2 * j + 2 < nblk

    @pl.when(has_next)
    def _():
        fetch(0, 2 * j + 2)

    await_x(1, 2 * j + 1)
    project(1)
    scan(0)
    emit(0)

    @pl.when(has_next)
    def _():
        fetch(1, 2 * j + 3)
        await_x(0, 2 * j + 2)

    project(0)
    scan(1)
    emit(1)


def _ssm_branch(x3d, g, w_ssm, are8, aim8, bbd, cre_bd, ncim_bd, d, glu_w, glu_b, *, tt):
    nb, seq, dm = x3d.shape
    r = tt * nb
    nblk = seq // tt
    assert nblk % 2 == 0
    return pl.pallas_call(
        functools.partial(_ssm_kernel, tt=tt, nb=nb, nblk=nblk),
        grid=(nblk // 2,),
        in_specs=[
            pl.BlockSpec(memory_space=pl.ANY),
            _resident(1, dm),
            _resident(dm, SSM_COLS),
            _resident(nb, SSM_CH),
            _resident(nb, SSM_CH),
            _resident(2, SSM_HALF_IN, 2 * SSM_HALF_CH),
            _resident(2, SSM_HALF_CH, SSM_HALF_IN),
            _resident(2, SSM_HALF_CH, SSM_HALF_IN),
            _resident(1, SSM_W),
            _resident(SSM_W, SSM_W),
            _resident(1, SSM_W),
        ],
        out_specs=pl.BlockSpec((2 * r, SSM_W), lambda i: (i, 0)),
        out_shape=jax.ShapeDtypeStruct((seq * nb, SSM_W), BF16),
        scratch_shapes=[
            pltpu.VMEM((2, tt, nb, dm), F32),
            pltpu.SemaphoreType.DMA((2, nb)),
            pltpu.VMEM((2, r, SSM_COLS), F32),
            pltpu.VMEM((2, r, SSM_CH), F32),
            pltpu.VMEM((2, r, SSM_CH), F32),
            pltpu.VMEM((2, nb, SSM_CH), F32),
        ],
        compiler_params=pltpu.CompilerParams(
            dimension_semantics=("arbitrary",),
            vmem_limit_bytes=V7X_VMEM_LIMIT_BYTES),
        name="ssm_branch",
    )(x3d, g, w_ssm, are8, aim8, bbd, cre_bd, ncim_bd, d, glu_w, glu_b)


def _ssm_zoh_kernel(lr_ref, li_ref, ls_ref, br_ref, bi_ref, are_ref, aim_ref, bbr_ref, bbi_ref):
    lr = lr_ref[...]
    li = li_ref[...]
    step = jnp.exp(ls_ref[...])
    mag = jnp.exp(step * lr)
    ang = step * li
    ab_re = mag * jnp.cos(ang)
    ab_im = mag * jnp.sin(ang)
    den = lr * lr + li * li
    nr = ab_re - 1.0
    ni = ab_im
    co_re = (nr * lr + ni * li) / den
    co_im = (ni * lr - nr * li) / den
    br = br_ref[...]
    bi = bi_ref[...]
    are_ref[...] = ab_re
    aim_ref[...] = ab_im
    bbr_ref[...] = co_re * br - co_im * bi
    bbi_ref[...] = co_re * bi + co_im * br


def _ssm_params(a_re, a_im, log_step, b_re, b_im, c_re, c_im, nb):
    flat = SSM_STATE * SSM_GROUP_CH
    rep = lambda a: jnp.repeat(a.astype(F32), SSM_GROUP_CH, axis=-1)
    out = jax.ShapeDtypeStruct((SSM_GROUPS, flat), F32)
    ab_re, ab_im, bb_re, bb_im = pl.pallas_call(
        _ssm_zoh_kernel, out_shape=(out, out, out, out), name="ssm_zoh",
    )(rep(a_re), rep(a_im),
      jnp.broadcast_to(log_step.astype(F32)[:, None], (SSM_GROUPS, flat)),
      b_re.astype(F32).reshape(SSM_GROUPS, flat), b_im.astype(F32).reshape(SSM_GROUPS, flat))
    ab_re = ab_re[:, ::SSM_GROUP_CH]
    ab_im = ab_im[:, ::SSM_GROUP_CH]
    bb_re = bb_re.reshape(SSM_GROUPS, SSM_STATE, SSM_GROUP_CH)
    bb_im = bb_im.reshape(SSM_GROUPS, SSM_STATE, SSM_GROUP_CH)
    gh = SSM_GROUPS // 2
    eye = jnp.eye(gh, dtype=F32)

    def b_blockdiag(bb):
        t = bb.reshape(2, gh, SSM_STATE, SSM_GROUP_CH)
        t = jnp.einsum('fgph,gk->fghkp', t, eye)
        return t.reshape(2, gh * SSM_GROUP_CH, gh * SSM_STATE)

    def c_blockdiag(c):
        t = c.astype(F32).reshape(2, gh, SSM_GROUP_CH, SSM_STATE)
        t = jnp.einsum('fghp,gk->fgpkh', t, eye)
        return t.reshape(2, gh * SSM_STATE, gh * SSM_GROUP_CH)

    bbd = jnp.concatenate([b_blockdiag(bb_re), b_blockdiag(bb_im)], axis=-1).astype(BF16)
    cre_bd = c_blockdiag(c_re).astype(BF16)
    ncim_bd = (-c_blockdiag(c_im)).astype(BF16)
    are8 = jnp.broadcast_to(ab_re.reshape(1, SSM_CH), (nb, SSM_CH))
    aim8 = jnp.broadcast_to(ab_im.reshape(1, SSM_CH), (nb, SSM_CH))
    return are8, aim8, bbd, cre_bd, ncim_bd


def _sgu_mix(u_ref, v_ref, gt_ref, lg_ref, lb_ref, w_ref, bt_ref, o_ref, tm):
    v = v_ref[...].astype(F32)
    mu = jnp.mean(v, axis=-1, keepdims=True)
    vc = v - mu
    var = jnp.mean(vc * vc, axis=-1, keepdims=True)
    vn = (vc * lax.rsqrt(var + LN_EPS) * lg_ref[...] + lb_ref[...]).astype(BF16)
    row = lax.broadcasted_iota(jnp.int32, (CHUNK, CHUNK), 0)
    col = lax.broadcasted_iota(jnp.int32, (CHUNK, CHUNK), 1)
    for hd in range(SGU_HEADS):
        w = jnp.where(col <= row, w_ref[hd], 0.0).astype(BF16)
        bias = bt_ref[:, hd:hd + 1]
        cs = slice(hd * SGU_HEAD_W, (hd + 1) * SGU_HEAD_W)
        for c in range(tm // CHUNK):
            rs = slice(c * CHUNK, (c + 1) * CHUNK)
            s = jnp.dot(w, vn[rs, cs], preferred_element_type=F32) + bias
            uu = u_ref[rs, cs].astype(F32)
            gg = gt_ref[rs, cs].astype(F32)
            o_ref[rs, cs] = (uu * s * _silu(gg)).astype(o_ref.dtype)


def _attn_kernel(q_ref, k_ref, v_ref, gt_ref, lq1_ref, lk1_ref, lq2_ref, lk2_ref, sg_ref,
                 o_ref, qzt_ref, vt_ref, s_ref, m_ref, acc_ref, fin_ref, *, tq, tk, lam_init):
    nkb = vt_ref.shape[0]
    nq = nkb // 2

    pad = lax.broadcasted_iota(jnp.int32, (ATT_PAD_ROWS, tk), 0)
    ones_row = jnp.where(pad == 0, 1.0, 0.0).astype(BF16)
    for j in range(nkb):
        vt_ref[j, :ATT_HEAD_W, :] = v_ref[j * tk:(j + 1) * tk, :].T
        vt_ref[j, ATT_HEAD_W:, :] = ones_row

    lam = (jnp.exp(jnp.sum(lq1_ref[...] * lk1_ref[...], axis=-1, keepdims=True))
           - jnp.exp(jnp.sum(lq2_ref[...] * lk2_ref[...], axis=-1, keepdims=True)) + lam_init)

    def q_rows(qi):
        return pl.ds(pl.multiple_of(qi * tq, tq), tq)

    def k_rows(kb):
        return pl.ds(pl.multiple_of(kb * tk, tk), tk)

    def prep_q(qi):
        qt = q_ref[q_rows(qi), :].T
        sub = lax.broadcasted_iota(jnp.int32, (ATT_HEAD_W, tq), 0)
        qzt_ref[0] = jnp.where(sub < ATT_HEAD_D, qt, 0).astype(BF16)
        qzt_ref[1] = jnp.where(sub >= ATT_HEAD_D, qt, 0).astype(BF16)

    def scores(st, kb, lo=0):
        s_ref[st, :, lo:] = jnp.dot(k_ref[k_rows(kb), :], qzt_ref[st, :, lo:],
                                    preferred_element_type=F32)

    def softmax_pv(st, kb, masked, lo=0):
        s = s_ref[st, :, lo:]
        if masked:
            key = lax.broadcasted_iota(jnp.int32, (tk, tq - lo), 0)
            qry = lax.broadcasted_iota(jnp.int32, (tk, tq - lo), 1)
            s = jnp.where(key <= qry, s, MASK_VALUE)
        m = m_ref[st, :, lo:]
        m_new = jnp.maximum(m, jnp.max(s, axis=0, keepdims=True))
        alpha = jnp.exp2(m - m_new)
        p = jnp.exp2(s - m_new).astype(BF16)
        acc_ref[st, :, lo:] = (alpha * acc_ref[st, :, lo:]
                               + jnp.dot(vt_ref[kb], p, preferred_element_type=F32))
        m_ref[st, :, lo:] = m_new

    def reset_stats():
        m_ref[...] = jnp.full(m_ref.shape, MASK_VALUE, F32)
        acc_ref[...] = jnp.zeros(acc_ref.shape, F32)

    def finish(qi):
        a1 = fin_ref[0]
        a2 = fin_ref[1]
        o1 = a1[:ATT_HEAD_W] * (1.0 / a1[ATT_HEAD_W:ATT_HEAD_W + 1])
        o2 = a2[:ATT_HEAD_W] * (1.0 / a2[ATT_HEAD_W:ATT_HEAD_W + 1])
        ot = o1 - lam * o2
        o = _rms(ot.T, sg_ref[...]) * (1.0 - lam_init)
        rows = q_rows(qi)
        o_ref[rows, :] = (o * _silu(gt_ref[rows, :].astype(F32))).astype(o_ref.dtype)

    prep_q(0)
    reset_stats()
    den = lax.broadcasted_iota(jnp.int32, fin_ref.shape, 1)
    fin_ref[...] = jnp.where(den == ATT_HEAD_W, 1.0, 0.0)
    scores(0, 0)

    def q_block(qi, c):
        def k_block(kb):
            scores(1, kb)
            softmax_pv(0, kb, False)
            scores(0, kb + 1)
            softmax_pv(1, kb, False)

        def k_pair(j, c2):
            k_block(2 * j)
            k_block(2 * j + 1)
            return c2

        lax.fori_loop(0, qi, k_pair, 0)
        da = 2 * qi
        scores(1, da)
        softmax_pv(0, da, True)
        scores(0, da + 1, tk)
        softmax_pv(1, da, True)
        scores(1, da + 1, tk)
        softmax_pv(0, da + 1, True, tk)
        prep_q(jnp.minimum(qi + 1, nq - 1))
        scores(0, 0)
        finish(jnp.maximum(qi - 1, 0))
        softmax_pv(1, da + 1, True, tk)
        fin_ref[...] = acc_ref[...]
        reset_stats()
        return c

    lax.fori_loop(0, nq, q_block, 0)
    finish(nq - 1)


def _attention(z_rest, lq1, lk1, lq2, lk2, subln_g, *, nbatch, seq, tk, lam_init):
    tq = 2 * tk
    const = lambda *shape: pl.BlockSpec(shape, lambda b, h: (0,) * len(shape))
    return pl.pallas_call(
        functools.partial(_attn_kernel, tq=tq, tk=tk, lam_init=lam_init),
        grid=(nbatch, ATT_HEADS),
        in_specs=[
            pl.BlockSpec((seq, ATT_HEAD_W), lambda b, h: (b, ATT_Q_OFF + h)),
            pl.BlockSpec((seq, ATT_HEAD_W), lambda b, h: (b, ATT_K_OFF + h)),
            pl.BlockSpec((seq, ATT_HEAD_W), lambda b, h: (b, ATT_V_OFF + h)),
            pl.BlockSpec((seq, ATT_HEAD_W), lambda b, h: (b, ATT_G_OFF + h)),
            const(1, ATT_HEAD_D),
            const(1, ATT_HEAD_D),
            const(1, ATT_HEAD_D),
            const(1, ATT_HEAD_D),
            const(1, ATT_HEAD_W),
        ],
        out_specs=pl.BlockSpec((seq, ATT_HEAD_W), lambda b, h: (b, h)),
        out_shape=jax.ShapeDtypeStruct((nbatch * seq, ATT_W), BF16),
        scratch_shapes=[pltpu.VMEM((2, ATT_HEAD_W, tq), BF16),
                        pltpu.VMEM((seq // tk, ATT_HEAD_W + ATT_PAD_ROWS, tk), BF16),
                        pltpu.VMEM((2, tk, tq), F32),
                        pltpu.VMEM((2, 1, tq), F32),
                        pltpu.VMEM((2, ATT_HEAD_W + ATT_PAD_ROWS, tq), F32),
                        pltpu.VMEM((2, ATT_HEAD_W + ATT_PAD_ROWS, tq), F32)],
        compiler_params=pltpu.CompilerParams(
            dimension_semantics=("parallel", "parallel"),
            vmem_limit_bytes=V7X_VMEM_LIMIT_BYTES),
        name="diff_attention",
    )(z_rest, z_rest, z_rest, z_rest, lq1, lk1, lq2, lk2, subln_g)


def _outproj_kernel(ys_ref, u_ref, v_ref, gt_ref, lg_ref, lb_ref, ws_ref, bt_ref, ya_ref, w_ref,
                    x_ref, fg_ref, o_ref, yg_ref, *, tm, final_norm):
    _sgu_mix(u_ref, v_ref, gt_ref, lg_ref, lb_ref, ws_ref, bt_ref, yg_ref, tm)
    acc = jnp.dot(ys_ref[...], w_ref[:SSM_W, :], preferred_element_type=F32)
    acc = acc + jnp.dot(yg_ref[...], w_ref[SSM_W:SSM_W + SGU_W, :], preferred_element_type=F32)
    acc = acc + jnp.dot(ya_ref[...], w_ref[SSM_W + SGU_W:, :], preferred_element_type=F32)
    xn = x_ref[...] + acc
    if final_norm:
        xn = _rms(xn, fg_ref[...])
    o_ref[...] = xn


def _outproj(y_ssm, z_rest, sgu_ln_g, sgu_ln_b, sgu_w, sgu_bt, y_att, w_out, x2d, final_g, *,
             tm, final_norm):
    m, d = x2d.shape
    assert tm % CHUNK == 0
    return pl.pallas_call(
        functools.partial(_outproj_kernel, tm=tm, final_norm=final_norm),
        grid=(m // tm,),
        in_specs=[
            pl.BlockSpec((tm, SSM_W), lambda i: (i, 0)),
            pl.BlockSpec((tm, SGU_W), lambda i: (i, SGU_U_BLK)),
            pl.BlockSpec((tm, SGU_W), lambda i: (i, SGU_V_BLK)),
            pl.BlockSpec((tm, SGU_W), lambda i: (i, SGU_G_BLK)),
            _resident(1, SGU_W),
            _resident(1, SGU_W),
            _resident(SGU_HEADS, CHUNK, CHUNK),
            _resident(CHUNK, SGU_HEADS),
            pl.BlockSpec((tm, ATT_W), lambda i: (i, 0)),
            _resident(D_MIX, d),
            pl.BlockSpec((tm, d), lambda i: (i, 0)),
            _resident(1, d),
        ],
        out_specs=pl.BlockSpec((tm, d), lambda i: (i, 0)),
        out_shape=jax.ShapeDtypeStruct((m, d), F32),
        scratch_shapes=[pltpu.VMEM((tm, SGU_W), BF16)],
        compiler_params=pltpu.CompilerParams(
            dimension_semantics=("parallel",),
            vmem_limit_bytes=V7X_VMEM_LIMIT_BYTES),
        name="outproj",
    )(y_ssm, z_rest, z_rest, z_rest, sgu_ln_g, sgu_ln_b, sgu_w, sgu_bt, y_att, w_out, x2d, final_g)


def _pick(n, prefs):
    for p in prefs:
        if n % p == 0:
            return p
    raise ValueError(f"no supported tile for extent {n}")


def kernel(x, norm_g, w_in, ssm_a_re, ssm_a_im, ssm_log_step, ssm_b_re, ssm_b_im, ssm_c_re, ssm_c_im,
           ssm_d, glu_w, glu_b, sgu_ln_g, sgu_ln_b, sgu_w, sgu_b, lam_q1, lam_k1, lam_q2, lam_k2,
           attn_subln_g, w_out, final_g):
    nbatch, seq, d = x.shape
    depth = norm_g.shape[0]
    assert d == D_MODEL and seq % 1024 == 0 and nbatch == 8
    m = nbatch * seq
    tm_in = _pick(m, (1024, 512, 256))
    tm_out = _pick(m, (1024, 512, 256))
    tt = _pick(seq, (64, 32))
    tk_att = 512
    q_lo = 3 * SGU_W
    rest_scale = jnp.ones((REST_COLS,), F32).at[q_lo:q_lo + ATT_W].set(ATT_HEAD_D ** -0.5 * LOG2_E)

    x2d = x.reshape(m, d).astype(F32)
    fg = final_g.reshape(1, d).astype(F32)
    for l in range(depth):
        g = norm_g[l].reshape(1, d).astype(F32)
        w_rest = (w_in[l, :, SSM_COLS:].astype(F32) * rest_scale).astype(BF16)
        w_ssm = w_in[l, :, :SSM_COLS].astype(BF16)
        lam_init = 0.8 - 0.6 * math.exp(-0.3 * l)

        z_rest = _inproj(x2d, g, w_rest, tm=tm_in)

        ssm_p = _ssm_params(ssm_a_re[l], ssm_a_im[l], ssm_log_step[l], ssm_b_re[l], ssm_b_im[l],
                            ssm_c_re[l], ssm_c_im[l], nbatch)
        y_ssm_tb = _ssm_branch(x2d.reshape(nbatch, seq, d), g, w_ssm, *ssm_p,
                               ssm_d[l].reshape(1, SSM_W).astype(F32), glu_w[l].astype(BF16),
                               glu_b[l].reshape(1, SSM_W).astype(F32), tt=tt)
        y_ssm = y_ssm_tb.reshape(seq, nbatch, SSM_W).swapaxes(0, 1).reshape(m, SSM_W)

        y_att = _attention(z_rest, lam_q1[l].reshape(1, -1).astype(F32), lam_k1[l].reshape(1, -1).astype(F32),
                           lam_q2[l].reshape(1, -1).astype(F32), lam_k2[l].reshape(1, -1).astype(F32),
                           attn_subln_g[l].reshape(1, ATT_HEAD_W).astype(F32),
                           nbatch=nbatch, seq=seq, tk=tk_att, lam_init=lam_init)

        x2d = _outproj(y_ssm, z_rest, sgu_ln_g[l].reshape(1, SGU_W).astype(F32),
                       sgu_ln_b[l].reshape(1, SGU_W).astype(F32), sgu_w[l].astype(F32),
                       sgu_b[l].astype(F32).T, y_att, w_out[l].astype(BF16), x2d, fg,
                       tm=tm_out, final_norm=(l == depth - 1))
    return x2d.reshape(nbatch, seq, d).astype(x.dtype)
```
